```python
import math
import jax, jax.numpy as jnp
from jax import lax
import numpy as np

D_MODEL = 1024
BATCH = 2
SEQ = 16384
DEPTH = 1
DEC_BATCH = 4
DEC_SEQ = 4096
PAST_LEN = 128

GDN_HEADS = 8
GDN_DK = 128
GDN_DV = 128
GDN_CONV = 5
GDN_CHUNK = 64
MLA_HEADS = 8
MLA_NOPE = 64
MLA_ROPE = 32
MLA_QK = MLA_NOPE + MLA_ROPE
MLA_V = 64
MLA_Q_LORA = 384
MLA_KV_LORA = 256
ROPE_THETA = 10000.0
Q_BLOCK = 128
MEM_TOKENS = 256
MEM_HEADS = 4
MEM_DH = 128
N_BRANCH = 3
PEER_HEADS = 8
PEER_DK = 256
PEER_KEYS = 128
PEER_TOPK = 16
PEER_EXPERTS = PEER_KEYS * PEER_KEYS
PEER_BLOCK = 128
NORM_EPS = 1e-6

IN_WIDTHS = (
    GDN_HEADS * GDN_DK,
    GDN_HEADS * GDN_DK,
    GDN_HEADS * GDN_DV,
    GDN_HEADS * GDN_DV,
    GDN_HEADS,
    GDN_HEADS,
    GDN_HEADS,
    GDN_HEADS,
    MLA_Q_LORA,
    MLA_KV_LORA,
    MLA_ROPE,
    MEM_HEADS * MEM_DH,
    N_BRANCH * D_MODEL,
)
IN_COLS = sum(IN_WIDTHS)

kernel_name = "hybrid_gdn_mla_mem_peer_encoder"

F32 = jnp.float32


def _split(t, widths):
    outs = []
    start = 0
    for w in widths:
        outs.append(t[..., start:start + w])
        start += w
    return outs


def _rmsnorm(x, g):
    xf = x.astype(F32)
    y = xf * lax.rsqrt(jnp.mean(xf * xf, axis=-1, keepdims=True) + NORM_EPS)
    return (y * g.astype(F32)).astype(x.dtype)


def _l2norm(x):
    xf = x.astype(F32)
    return (xf * lax.rsqrt(jnp.sum(xf * xf, axis=-1, keepdims=True) + NORM_EPS)).astype(x.dtype)


def _centred_dwconv(x, w):
    c = x.shape[-1]
    pad = GDN_CONV // 2
    return lax.conv_general_dilated(x, w[:, None, :].astype(x.dtype), window_strides=(1,),
                                    padding=[(pad, pad)], dimension_numbers=('NWC', 'WIO', 'NWC'),
                                    feature_group_count=c)


def _gdn_gates(a, b, a_log, dt_bias):
    g = -jnp.exp(a_log.astype(F32)) * jax.nn.softplus(a.astype(F32) + dt_bias.astype(F32))
    beta = jax.nn.sigmoid(b.astype(F32))
    return g, beta


def _gated_delta_chunked(q, k, v, g, beta):
    B, S, H, DK = q.shape
    DV = v.shape[-1]
    C = GDN_CHUNK
    n = S // C

    def chunks(t):
        t = t.astype(F32).reshape((B, n, C, H) + t.shape[3:])
        return jnp.moveaxis(t, (1, 3), (0, 2))

    qc = chunks(q) * (DK ** -0.5)
    kc = chunks(k)
    vc = chunks(v)
    bc = chunks(beta)
    gc = jnp.cumsum(chunks(g), axis=-1)
    idx = jnp.arange(C)
    lower = idx[:, None] >= idx[None, :]
    strict = idx[:, None] > idx[None, :]
    decay = jnp.exp(jnp.where(lower, gc[..., :, None] - gc[..., None, :], -jnp.inf))
    kb = kc * bc[..., None]
    lmat = jnp.where(strict, jnp.einsum('nbhid,nbhjd->nbhij', kb, kc) * decay, 0.0)
    amat = lmat + jnp.eye(C, dtype=F32)
    rhs = jnp.concatenate([vc * bc[..., None], kb * jnp.exp(gc)[..., None]], axis=-1)
    sol = lax.linalg.triangular_solve(amat, rhs, left_side=True, lower=True, unit_diagonal=True)
    u_c, w_c = sol[..., :DV], sol[..., DV:]
    intra = jnp.where(lower, jnp.einsum('nbhid,nbhjd->nbhij', qc, kc) * decay, 0.0)
    q_dec = qc * jnp.exp(gc)[..., None]
    k_dec = kc * jnp.exp(gc[..., -1:] - gc)[..., None]
    g_last = jnp.exp(gc[..., -1])

    def step(state, xs):
        intra_i, u_i, w_i, qd_i, kd_i, gl_i = xs
        v_new = u_i - jnp.einsum('bhcd,bhde->bhce', w_i, state)
        o_i = jnp.einsum('bhcd,bhde->bhce', qd_i, state) + jnp.einsum('bhij,bhje->bhie', intra_i, v_new)
        state = state * gl_i[..., None, None] + jnp.einsum('bhcd,bhce->bhde', kd_i, v_new)
        return state, o_i

    state0 = jnp.zeros((B, H, DK, DV), F32)
    _, o = lax.scan(step, state0, (intra, u_c, w_c, q_dec, k_dec, g_last))
    return jnp.moveaxis(o, (0, 2), (1, 3)).reshape(B, S, H, DV)


def _rope(x, pos):
    half = MLA_ROPE // 2
    inv = ROPE_THETA ** (-jnp.arange(half, dtype=F32) / half)
    ang = pos.astype(F32)[:, None] * inv[None, :]
    cos = jnp.cos(ang)[:, None, :]
    sin = jnp.sin(ang)[:, None, :]
    xf = x.astype(F32)
    x1, x2 = xf[..., :half], xf[..., half:]
    return jnp.concatenate([x1 * cos - x2 * sin, x2 * cos + x1 * sin], axis=-1).astype(x.dtype)


def _blocked_attention(q, k, v):
    B, S, H, Dh = q.shape
    scale = Dh ** -0.5
    qb = jnp.moveaxis(q.reshape(B, S // Q_BLOCK, Q_BLOCK, H, Dh), 1, 0)

    def one(qi):
        s = jnp.einsum('bqhd,bkhd->bhqk', qi, k).astype(F32) * scale
        p = jax.nn.softmax(s, axis=-1).astype(v.dtype)
        return jnp.einsum('bhqk,bkhd->bqhd', p, v)

    o = lax.map(one, qb)
    return jnp.moveaxis(o, 0, 1).reshape(B, S, H, v.shape[-1])


def _peer(z, w_q, keys_a, keys_b, u_tab, v_tab):
    B, S, D = z.shape
    half = PEER_DK // 2
    zt = z.reshape(-1, PEER_BLOCK, D)

    def blk(zb):
        q = (zb @ w_q).reshape(PEER_BLOCK, PEER_HEADS, PEER_DK)
        s1 = jnp.einsum('thd,hnd->thn', q[..., :half], keys_a)
        s2 = jnp.einsum('thd,hnd->thn', q[..., half:], keys_b)
        v1, i1 = lax.top_k(s1, PEER_TOPK)
        v2, i2 = lax.top_k(s2, PEER_TOPK)
        cand = (v1[..., :, None] + v2[..., None, :]).reshape(PEER_BLOCK, PEER_HEADS, PEER_TOPK * PEER_TOPK)
        cidx = (i1[..., :, None] * PEER_KEYS + i2[..., None, :]).reshape(PEER_BLOCK, PEER_HEADS, PEER_TOPK * PEER_TOPK)
        top_s, pos = lax.top_k(cand, PEER_TOPK)
        eidx = jnp.take_along_axis(cidx, pos, axis=-1)
        gate = jax.nn.softmax(top_s.astype(F32), axis=-1).astype(zb.dtype)
        u = u_tab[eidx]
        act = jax.nn.gelu(jnp.einsum('thkd,td->thk', u, zb), approximate=False)
        return jnp.einsum('thk,thkd->td', gate * act, v_tab[eidx])

    return lax.map(blk, zt).reshape(B, S, D)


def _encoder_layer(x, mem, g_mix, w_in, gdn_conv_w, gdn_a_log_fwd, gdn_a_log_bwd, gdn_dt_bias_fwd,
                   gdn_dt_bias_bwd, gdn_out_norm, mla_q_norm, mla_w_uq, mla_kv_norm, mla_w_ukv,
                   mla_qk_norm_q, mla_qk_norm_k, mem_norm, mem_w_kv, mem_qk_norm_q, mem_qk_norm_k,
                   w_branch_gdn, w_branch_mla, w_branch_mem, w_out, g_ffn, peer_w_q, peer_keys_a,
                   peer_keys_b, peer_u, peer_v):
    B, S, _ = x.shape
    h = _rmsnorm(x, g_mix)
    (gq, gk, gv, gz, a_f, a_b, b_f, b_b, cq, ckv, kpe, mq, gate_in) = _split(h @ w_in, IN_WIDTHS)

    qkv = jax.nn.silu(_centred_dwconv(jnp.concatenate([gq, gk, gv], axis=-1), gdn_conv_w))
    q, k, v = _split(qkv, (GDN_HEADS * GDN_DK, GDN_HEADS * GDN_DK, GDN_HEADS * GDN_DV))
    q = _l2norm(q.reshape(B, S, GDN_HEADS, GDN_DK))
    k = _l2norm(k.reshape(B, S, GDN_HEADS, GDN_DK))
    v = v.reshape(B, S, GDN_HEADS, GDN_DV)
    g_f, beta_f = _gdn_gates(a_f, b_f, gdn_a_log_fwd, gdn_dt_bias_fwd)
    g_b, beta_b = _gdn_gates(a_b, b_b, gdn_a_log_bwd, gdn_dt_bias_bwd)
    o_f = _gated_delta_chunked(q, k, v, g_f, beta_f)
    o_b = jnp.flip(_gated_delta_chunked(jnp.flip(q, 1), jnp.flip(k, 1), jnp.flip(v, 1),
                                        jnp.flip(g_b, 1), jnp.flip(beta_b, 1)), 1)
    o_gdn = _rmsnorm(o_f + o_b, gdn_out_norm) * jax.nn.silu(gz.reshape(B, S, GDN_HEADS, GDN_DV).astype(F32))
    y_gdn = o_gdn.astype(x.dtype).reshape(B, S, GDN_HEADS * GDN_DV) @ w_branch_gdn

    q_m = (_rmsnorm(cq, mla_q_norm) @ mla_w_uq).reshape(B, S, MLA_HEADS, MLA_QK)
    kv_m = (_rmsnorm(ckv, mla_kv_norm) @ mla_w_ukv).reshape(B, S, MLA_HEADS, MLA_NOPE + MLA_V)
    k_nope, v_m = kv_m[..., :MLA_NOPE], kv_m[..., MLA_NOPE:]
    k_pe = jnp.broadcast_to(kpe[:, :, None, :], (B, S, MLA_HEADS, MLA_ROPE))
    k_m = jnp.concatenate([k_nope, k_pe], axis=-1)
    q_m = _rmsnorm(q_m, mla_qk_norm_q)
    k_m = _rmsnorm(k_m, mla_qk_norm_k)
    pos = jnp.arange(S)
    q_m = jnp.concatenate([q_m[..., :MLA_NOPE], _rope(q_m[..., MLA_NOPE:], pos)], axis=-1)
    k_m = jnp.concatenate([k_m[..., :MLA_NOPE], _rope(k_m[..., MLA_NOPE:], pos)], axis=-1)
    o_mla = _blocked_attention(q_m, k_m, v_m)
    y_mla = o_mla.reshape(B, S, MLA_HEADS * MLA_V) @ w_branch_mla

    mkv = _rmsnorm(mem, mem_norm) @ mem_w_kv
    M = mem.shape[1]
    mk = _rmsnorm(mkv[..., :MEM_HEADS * MEM_DH].reshape(B, M, MEM_HEADS, MEM_DH), mem_qk_norm_k)
    mv = mkv[..., MEM_HEADS * MEM_DH:].reshape(B, M, MEM_HEADS, MEM_DH)
    mqh = _rmsnorm(mq.reshape(B, S, MEM_HEADS, MEM_DH), mem_qk_norm_q)
    s_mem = jnp.einsum('bshd,bmhd->bhsm', mqh, mk).astype(F32) * (MEM_DH ** -0.5)
    p_mem = jax.nn.softmax(s_mem, axis=-1).astype(mv.dtype)
    o_mem = jnp.einsum('bhsm,bmhd->bshd', p_mem, mv)
    y_mem = o_mem.reshape(B, S, MEM_HEADS * MEM_DH) @ w_branch_mem

    gates = jax.nn.sigmoid(gate_in).reshape(B, S, N_BRANCH, D_MODEL)
    merged = gates[:, :, 0] * y_gdn + gates[:, :, 1] * y_mla + gates[:, :, 2] * y_mem
    x = x + merged @ w_out

    x = x + _peer(_rmsnorm(x, g_ffn), peer_w_q, peer_keys_a, peer_keys_b, peer_u, peer_v)
    return x


def setup_inputs(seed: int = 0) -> dict:
    key = jax.random.key(seed)
    ks = jax.random.split(key, 40)

    def nrm(k, shape, scale):
        return jax.random.normal(k, shape, F32) * scale

    def gain(k, n):
        return 1.0 + 0.02 * jax.random.normal(k, (n,), F32)

    dt = jnp.exp(jax.random.uniform(ks[8], (2, GDN_HEADS), F32, math.log(1e-3), math.log(1e-1)))
    dt_bias = dt + jnp.log(-jnp.expm1(-dt))
    a_log = jnp.log(jax.random.uniform(ks[9], (2, GDN_HEADS), F32, 1.0, 16.0))
    return {
        "x_prompt": nrm(ks[0], (BATCH, SEQ, D_MODEL), 1.0),
        "x_sample": nrm(ks[1], (DEC_BATCH, DEC_SEQ, D_MODEL), 1.0),
        "mem_prompt": nrm(ks[2], (BATCH, MEM_TOKENS, D_MODEL), 1.0),
        "mem_sample": nrm(ks[3], (DEC_BATCH, MEM_TOKENS, D_MODEL), 1.0),
        "g_mix": gain(ks[4], D_MODEL),
        "w_in": nrm(ks[5], (D_MODEL, IN_COLS), D_MODEL ** -0.5),
        "gdn_conv_w": nrm(ks[6], (GDN_CONV, 2 * GDN_HEADS * GDN_DK + GDN_HEADS * GDN_DV), GDN_CONV ** -0.5),
        "gdn_a_log_fwd": a_log[0],
        "gdn_a_log_bwd": a_log[1],
        "gdn_dt_bias_fwd": dt_bias[0],
        "gdn_dt_bias_bwd": dt_bias[1],
        "gdn_out_norm": gain(ks[10], GDN_DV),
        "mla_q_norm": gain(ks[11], MLA_Q_LORA),
        "mla_w_uq": nrm(ks[12], (MLA_Q_LORA, MLA_HEADS * MLA_QK), MLA_Q_LORA ** -0.5),
        "mla_kv_norm": gain(ks[13], MLA_KV_LORA),
        "mla_w_ukv": nrm(ks[14], (MLA_KV_LORA, MLA_HEADS * (MLA_NOPE + MLA_V)), MLA_KV_LORA ** -0.5),
        "mla_qk_norm_q": gain(ks[15], MLA_QK),
        "mla_qk_norm_k": gain(ks[16], MLA_QK),
        "mem_norm": gain(ks[17], D_MODEL),
        "mem_w_kv": nrm(ks[18], (D_MODEL, 2 * MEM_HEADS * MEM_DH), D_MODEL ** -0.5),
        "mem_qk_norm_q": gain(ks[19], MEM_DH),
        "mem_qk_norm_k": gain(ks[20], MEM_DH),
        "w_branch_gdn": nrm(ks[21], (GDN_HEADS * GDN_DV, D_MODEL), (GDN_HEADS * GDN_DV) ** -0.5),
        "w_branch_mla": nrm(ks[22], (MLA_HEADS * MLA_V, D_MODEL), (MLA_HEADS * MLA_V) ** -0.5),
        "w_branch_mem": nrm(ks[23], (MEM_HEADS * MEM_DH, D_MODEL), (MEM_HEADS * MEM_DH) ** -0.5),
        "w_out": nrm(ks[24], (D_MODEL, D_MODEL), D_MODEL ** -0.5),
        "g_ffn": gain(ks[25], D_MODEL),
        "peer_w_q": nrm(ks[26], (D_MODEL, PEER_HEADS * PEER_DK), D_MODEL ** -0.5),
        "peer_keys_a": nrm(ks[27], (PEER_HEADS, PEER_KEYS, PEER_DK // 2), (PEER_DK // 2) ** -0.5),
        "peer_keys_b": nrm(ks[28], (PEER_HEADS, PEER_KEYS, PEER_DK // 2), (PEER_DK // 2) ** -0.5),
        "peer_u": nrm(ks[29], (PEER_EXPERTS, D_MODEL), D_MODEL ** -0.5),
        "peer_v": nrm(ks[30], (PEER_EXPERTS, D_MODEL), 0.5),
    }


def reference(x_prompt, x_sample, mem_prompt, mem_sample, g_mix, w_in, gdn_conv_w, gdn_a_log_fwd,
              gdn_a_log_bwd, gdn_dt_bias_fwd, gdn_dt_bias_bwd, gdn_out_norm, mla_q_norm, mla_w_uq,
              mla_kv_norm, mla_w_ukv, mla_qk_norm_q, mla_qk_norm_k, mem_norm, mem_w_kv, mem_qk_norm_q,
              mem_qk_norm_k, w_branch_gdn, w_branch_mla, w_branch_mem, w_out, g_ffn, peer_w_q,
              peer_keys_a, peer_keys_b, peer_u, peer_v):
    weights = (g_mix, w_in, gdn_conv_w, gdn_a_log_fwd, gdn_a_log_bwd, gdn_dt_bias_fwd, gdn_dt_bias_bwd,
               gdn_out_norm, mla_q_norm, mla_w_uq, mla_kv_norm, mla_w_ukv, mla_qk_norm_q, mla_qk_norm_k,
               mem_norm, mem_w_kv, mem_qk_norm_q, mem_qk_norm_k, w_branch_gdn, w_branch_mla, w_branch_mem,
               w_out, g_ffn, peer_w_q, peer_keys_a, peer_keys_b, peer_u, peer_v)
    y_prompt = x_prompt
    y_sample = x_sample
    for _ in range(DEPTH):
        y_prompt = _encoder_layer(y_prompt, mem_prompt, *weights)
        y_sample = _encoder_layer(y_sample, mem_sample, *weights)
    return (y_prompt, y_sample)
```

```python
import functools

import jax
import jax.numpy as jnp
from jax import lax
from jax.experimental import pallas as pl
from jax.experimental.pallas import tpu as pltpu

F32 = jnp.float32
BF16 = jnp.bfloat16
I32 = jnp.int32

D_MODEL = 1024
GDN_HEADS = 8
GDN_DK = 128
GDN_DV = 128
GDN_CONV = 5
GDN_CHUNK = 64
MLA_HEADS = 8
MLA_NOPE = 64
MLA_ROPE = 32
MLA_QK = MLA_NOPE + MLA_ROPE
MLA_V = 64
MLA_Q_LORA = 384
MLA_KV_LORA = 256
ROPE_THETA = 10000.0
MEM_TOKENS = 256
MEM_HEADS = 4
MEM_DH = 128
PEER_HEADS = 8
PEER_DK = 256
PEER_KEYS = 128
PEER_TOPK = 16
NORM_EPS = 1e-6

LANES = 128
MIB = 1024 * 1024

C_Q, C_K, C_V, C_Z = 0, 1024, 2048, 3072
C_GATE = 4096
C_MQ = 7168
C_LAT = 7680
N_MAIN = 8320
L_AF, L_AB, L_BF, L_BB, L_KPE = 0, 8, 16, 24, 32

PEER_TILE = 128
PEER_ROWS = PEER_HEADS * PEER_TOPK
PEER_BUFS = 4


def _params(sem, vmem_mib):
    return pltpu.CompilerParams(dimension_semantics=sem, vmem_limit_bytes=vmem_mib * MIB)


def _mm(a, b):
    return jnp.dot(a.astype(BF16), b.astype(BF16), preferred_element_type=F32)


def _mm_nt(a, b):
    return lax.dot_general(a.astype(BF16), b.astype(BF16), (((1,), (1,)), ((), ())),
                           preferred_element_type=F32)


def _mm_tn(a, b):
    return lax.dot_general(a.astype(BF16), b.astype(BF16), (((0,), (0,)), ((), ())),
                           preferred_element_type=F32)


def _rms(x, n=None):
    n = x.shape[-1] if n is None else n
    return x * lax.rsqrt(jnp.sum(x * x, axis=-1, keepdims=True) * (1.0 / n) + NORM_EPS)


def _norm_proj_kernel(has_small, x_ref, g_ref, w_ref, *rest):
    if has_small:
        ws_ref, o_ref, os_ref, h_scr = rest
    else:
        o_ref, h_scr = rest

    @pl.when(pl.program_id(1) == 0)
    def _():
        h = _rms(x_ref[...]) * g_ref[...]
        h_scr[...] = h.astype(BF16)
        if has_small:
            os_ref[...] = jnp.dot(h_scr[...], ws_ref[...], preferred_element_type=F32)

    o_ref[...] = jnp.dot(h_scr[...], w_ref[...], preferred_element_type=F32).astype(o_ref.dtype)


def _norm_proj(x2d, g, w, w_small, n_col_blocks, out_dtype):
    t, d = x2d.shape
    n = w.shape[1]
    tm = min(1024, t)
    tn = n // n_col_blocks
    in_specs = [pl.BlockSpec((tm, d), lambda i, j: (i, 0)),
                pl.BlockSpec((1, d), lambda i, j: (0, 0)),
                pl.BlockSpec((d, tn), lambda i, j: (0, j))]
    out_shape = [jax.ShapeDtypeStruct((t, n), out_dtype)]
    out_specs = [pl.BlockSpec((tm, tn), lambda i, j: (i, j))]
    args = [x2d, g.reshape(1, d), w]
    if w_small is not None:
        in_specs.append(pl.BlockSpec((d, LANES), lambda i, j: (0, 0)))
        out_shape.append(jax.ShapeDtypeStruct((t, LANES), F32))
        out_specs.append(pl.BlockSpec((tm, LANES), lambda i, j: (i, 0)))
        args.append(w_small)
    return pl.pallas_call(
        functools.partial(_norm_proj_kernel, w_small is not None),
        grid=(t // tm, n_col_blocks),
        in_specs=in_specs, out_specs=out_specs, out_shape=out_shape,
        scratch_shapes=[pltpu.VMEM((tm, d), BF16)],
        compiler_params=_params(("parallel", "arbitrary"), 48),
        name="norm_proj",
    )(*args)


def _gdn_conv_kernel(ts, x_ref, w_ref, o_ref):
    s = x_ref.shape[1]
    n_chunks = s // ts
    halo = 16
    n_ext = ts + 2 * halo
    is_qk = pl.program_id(1) < 2 * GDN_HEADS
    w = w_ref[...]

    def body(i, carry):
        t0 = pl.multiple_of(i * ts, ts)
        main = x_ref[0, pl.ds(t0, ts), :].astype(F32)
        p0 = pl.multiple_of(jnp.maximum(t0 - halo, 0), halo)
        n0 = pl.multiple_of(jnp.minimum(t0 + ts, s - halo), halo)
        prev = jnp.where(i > 0, x_ref[0, pl.ds(p0, halo), :].astype(F32), 0.0)
        nxt = jnp.where(i < n_chunks - 1, x_ref[0, pl.ds(n0, halo), :].astype(F32), 0.0)
        ext = jnp.concatenate([prev, main, nxt], axis=0)
        acc = w[2:3, :] * ext
        for j in (0, 1, 3, 4):
            acc = acc + w[j:j + 1, :] * pltpu.roll(ext, (2 - j) % n_ext, 0)
        y = acc[halo:halo + ts]
        y = y * jax.nn.sigmoid(y)
        y_n = y * lax.rsqrt(jnp.sum(y * y, axis=-1, keepdims=True) + NORM_EPS)
        o_ref[0, 0, pl.ds(t0, ts), :] = jnp.where(is_qk, y_n, y).astype(o_ref.dtype)
        return carry

    lax.fori_loop(0, n_chunks, body, 0)


def _gdn_conv(p3, conv_w):
    b, s, _ = p3.shape
    ts = min(512, s)
    n_blk = 3 * GDN_HEADS
    return pl.pallas_call(
        functools.partial(_gdn_conv_kernel, ts),
        grid=(b, n_blk),
        in_specs=[pl.BlockSpec((1, s, LANES), lambda bi, c: (bi, 0, c)),
                  pl.BlockSpec((GDN_CONV, LANES), lambda bi, c: (0, c))],
        out_specs=pl.BlockSpec((1, 1, s, LANES), lambda bi, c: (bi, c, 0, 0)),
        out_shape=jax.ShapeDtypeStruct((b, n_blk, s, LANES), BF16),
        compiler_params=_params(("parallel", "parallel"), 40),
        name="gdn_conv",
    )(p3, conv_w)


def _gdn_scan_kernel(reverse, hb, ts, *refs):
    if reverse:
        q_ref, k_ref, v_ref, ps_ref, al_ref, dtb_ref, of_ref, z_ref, gn_ref, o_ref, s_scr = refs
    else:
        q_ref, k_ref, v_ref, ps_ref, al_ref, dtb_ref, o_ref, s_scr = refs
    c = GDN_CHUNK
    n_chunks = ts // c

    @pl.when(pl.program_id(2) == 0)
    def _():
        s_scr[...] = jnp.zeros_like(s_scr)

    ps = ps_ref[0]
    g = -al_ref[...] * jax.nn.softplus(ps + dtb_ref[...])
    beta = jax.nn.sigmoid(ps)
    rowid = lax.broadcasted_iota(I32, (ts, LANES), 0) & (c - 1)
    gc = g
    for kk in (1, 2, 4, 8, 16, 32):
        if reverse:
            gc = gc + jnp.where(rowid < c - kk, pltpu.roll(gc, ts - kk, 0), 0.0)
        else:
            gc = gc + jnp.where(rowid >= kk, pltpu.roll(gc, kk, 0), 0.0)
    shift = (LANES - pl.program_id(1) * hb) % LANES
    gc = pltpu.roll(gc, shift, 1)
    beta = pltpu.roll(beta, shift, 1)
    eg = jnp.exp(gc)
    gc_t = [gc[p * LANES:(p + 1) * LANES, :].T for p in range(ts // LANES)]

    r = lax.broadcasted_iota(I32, (c, c), 0)
    cc = lax.broadcasted_iota(I32, (c, c), 1)
    if reverse:
        r, cc = cc, r
    incl = r >= cc
    strict = r > cc
    eye = jnp.where(r == cc, 1.0, 0.0)

    def level_mask(log_s):
        rb = r >> log_s
        return ((rb & 1) == 1) & ((cc >> log_s) == rb - 1)

    lane_g = L_AB if reverse else L_AF
    lane_b = L_BB if reverse else L_BF
    scale = GDN_DK ** -0.5
    order = range(n_chunks - 1, -1, -1) if reverse else range(n_chunks)
    for ci in order:
        r0 = ci * c
        gt = gc_t[r0 // LANES]
        l0 = r0 % LANES
        r_last = r0 if reverse else r0 + c - 1
        for hh in range(hb):
            q = q_ref[0, hh, r0:r0 + c, :].astype(F32) * scale
            k = k_ref[0, hh, r0:r0 + c, :].astype(F32)
            v = v_ref[0, hh, r0:r0 + c, :].astype(F32)
            la = lane_g + hh
            lb = lane_b + hh
            gc_c = gc[r0:r0 + c, la:la + 1]
            eg_c = eg[r0:r0 + c, la:la + 1]
            b_c = beta[r0:r0 + c, lb:lb + 1]
            gc_r = gt[la:la + 1, l0:l0 + c]
            gc_l = gc[r_last:r_last + 1, la:la + 1]
            dec = jnp.exp(jnp.where(incl, gc_c - gc_r, -1e30))
            kb = k * b_c
            kq = _mm_nt(jnp.concatenate([kb, q], axis=0), k)
            lm = jnp.where(strict, kq[:c] * dec, 0.0)
            intra = jnp.where(incl, kq[c:] * dec, 0.0)
            dm = eye - jnp.where(level_mask(0), lm, 0.0)
            for log_s in (1, 2, 3, 4, 5):
                dm = dm - _mm(dm, _mm(jnp.where(level_mask(log_s), lm, 0.0), dm))
            sol = _mm(dm, jnp.concatenate([v * b_c, kb * eg_c], axis=1))
            u = sol[:, :GDN_DV]
            w = sol[:, GDN_DV:]
            st = s_scr[hh]
            ws_qs = _mm(jnp.concatenate([w, q * eg_c], axis=0), st)
            v_new = u - ws_qs[:c]
            o = ws_qs[c:] + _mm(intra, v_new)
            kd = k * jnp.exp(gc_l - gc_c)
            s_scr[hh] = st * jnp.exp(gc_l) + _mm_tn(kd, v_new)
            cols = slice(hh * GDN_DV, (hh + 1) * GDN_DV)
            if reverse:
                tot = o + of_ref[0, r0:r0 + c, cols]
                zz = z_ref[0, r0:r0 + c, cols].astype(F32)
                y = _rms(tot) * gn_ref[...] * (zz * jax.nn.sigmoid(zz))
                o_ref[0, r0:r0 + c, cols] = y.astype(o_ref.dtype)
            else:
                o_ref[0, r0:r0 + c, cols] = o


def _gdn_scan(qkv, ps3, a_lane, dtb_lane, reverse, o_fwd=None, p3=None, out_norm=None):
    b, _, s, _ = qkv.shape
    hb = 2
    ts = 256
    nt = s // ts
    n_hblk = GDN_HEADS // hb
    tile = (lambda i: nt - 1 - i) if reverse else (lambda i: i)
    in_specs = [
        pl.BlockSpec((1, hb, ts, LANES), lambda bi, h, i: (bi, h, tile(i), 0)),
        pl.BlockSpec((1, hb, ts, LANES), lambda bi, h, i: (bi, n_hblk + h, tile(i), 0)),
        pl.BlockSpec((1, hb, ts, LANES), lambda bi, h, i: (bi, 2 * n_hblk + h, tile(i), 0)),
        pl.BlockSpec((1, ts, LANES), lambda bi, h, i: (bi, tile(i), 0)),
        pl.BlockSpec((1, LANES), lambda bi, h, i: (0, 0)),
        pl.BlockSpec((1, LANES), lambda bi, h, i: (0, 0)),
    ]
    args = [qkv, qkv, qkv, ps3, a_lane, dtb_lane]
    wide = hb * GDN_DV
    if reverse:
        in_specs += [
            pl.BlockSpec((1, ts, wide), lambda bi, h, i: (bi, tile(i), h)),
            pl.BlockSpec((1, ts, wide), lambda bi, h, i: (bi, tile(i), C_Z // wide + h)),
            pl.BlockSpec((1, GDN_DV), lambda bi, h, i: (0, 0)),
        ]
        args += [o_fwd, p3, out_norm.reshape(1, GDN_DV)]
    out_dtype = BF16 if reverse else F32
    return pl.pallas_call(
        functools.partial(_gdn_scan_kernel, reverse, hb, ts),
        grid=(b, n_hblk, nt),
        in_specs=in_specs,
        out_specs=pl.BlockSpec((1, ts, wide), lambda bi, h, i: (bi, tile(i), h)),
        out_shape=jax.ShapeDtypeStruct((b, s, GDN_HEADS * GDN_DV), out_dtype),
        scratch_shapes=[pltpu.VMEM((hb, GDN_DK, GDN_DV), F32)],
        compiler_params=_params(("parallel", "parallel", "arbitrary"), 32),
        name="gdn_scan_bwd" if reverse else "gdn_scan_fwd",
    )(*args)


def _mla_prep_kernel(lat_ref, ps_ref, cos_ref, sa_ref, sb_ref, gq_ref, gkv_ref, wuq_ref, wuk_ref,
                     wuv_ref, nq_ref, nk_ref, q_ref, k_ref, v_ref):
    lat = lat_ref[0].astype(F32)
    cq = _rms(lat[:, :MLA_Q_LORA]) * gq_ref[...]
    ckv = _rms(lat[:, MLA_Q_LORA:]) * gkv_ref[...]
    q_all = _mm(cq, wuq_ref[...])
    k_all = _mm(ckv, wuk_ref[...])
    v_ref[0] = _mm(ckv, wuv_ref[...]).astype(v_ref.dtype)
    ps = ps_ref[0]
    lane = lax.broadcasted_iota(I32, ps.shape, 1)
    in_rope = (lane >= MLA_NOPE) & (lane < MLA_QK)
    kpe = jnp.where(in_rope, pltpu.roll(ps, MLA_NOPE - L_KPE, 1), 0.0)
    cos = cos_ref[...]
    sin_a = sa_ref[...]
    sin_b = sb_ref[...]
    half = MLA_ROPE // 2

    def norm_rope(x, gain):
        y = _rms(x, MLA_QK) * gain
        return y * cos + pltpu.roll(y, half, 1) * sin_a + pltpu.roll(y, LANES - half, 1) * sin_b

    for h in range(MLA_HEADS):
        cols = slice(h * LANES, (h + 1) * LANES)
        q_ref[0, h] = (norm_rope(q_all[:, cols], nq_ref[...]) * (MLA_QK ** -0.5)).astype(q_ref.dtype)
        k_ref[0, h] = norm_rope(k_all[:, cols] + kpe, nk_ref[...]).astype(k_ref.dtype)


def _mla_prep(p3, ps3, rope_tabs, wts):
    b, s, _ = p3.shape
    tm = min(512, s)
    lat_w = MLA_Q_LORA + MLA_KV_LORA
    full = lambda shape: pl.BlockSpec(shape, lambda bi, i: tuple(0 for _ in shape))
    tab = pl.BlockSpec((tm, LANES), lambda bi, i: (i, 0))
    hw = MLA_HEADS * LANES
    return pl.pallas_call(
        _mla_prep_kernel,
        grid=(b, s // tm),
        in_specs=[pl.BlockSpec((1, tm, lat_w), lambda bi, i: (bi, i, C_LAT // lat_w)),
                  pl.BlockSpec((1, tm, LANES), lambda bi, i: (bi, i, 0)),
                  tab, tab, tab,
                  full((1, MLA_Q_LORA)), full((1, MLA_KV_LORA)),
                  full((MLA_Q_LORA, hw)), full((MLA_KV_LORA, hw)),
                  full((MLA_KV_LORA, MLA_HEADS * MLA_V)),
                  full((1, LANES)), full((1, LANES))],
        out_specs=[pl.BlockSpec((1, MLA_HEADS, tm, LANES), lambda bi, i: (bi, 0, i, 0)),
                   pl.BlockSpec((1, MLA_HEADS, tm, LANES), lambda bi, i: (bi, 0, i, 0)),
                   pl.BlockSpec((1, tm, MLA_HEADS * MLA_V), lambda bi, i: (bi, i, 0))],
        out_shape=[jax.ShapeDtypeStruct((b, MLA_HEADS, s, LANES), BF16),
                   jax.ShapeDtypeStruct((b, MLA_HEADS, s, LANES), BF16),
                   jax.ShapeDtypeStruct((b, s, MLA_HEADS * MLA_V), BF16)],
        compiler_params=_params(("parallel", "parallel"), 40),
        name="mla_prep",
    )(p3, ps3, *rope_tabs, wts["mla_q_norm"], wts["mla_kv_norm"], wts["w_uq"], wts["w_uk"],
      wts["w_uv"], wts["qk_norm_q"], wts["qk_norm_k"])


def _attn_kernel(q_ref, k_ref, v_ref, o_ref, m_scr, l_scr, acc_scr):
    j = pl.program_id(3)

    @pl.when(j == 0)
    def _():
        m_scr[...] = jnp.full_like(m_scr, -jnp.inf)
        l_scr[...] = jnp.zeros_like(l_scr)
        acc_scr[...] = jnp.zeros_like(acc_scr)

    v = v_ref[0]
    for hh in range(2):
        s = lax.dot_general(q_ref[0, hh], k_ref[0, hh], (((1,), (1,)), ((), ())),
                            preferred_element_type=F32)
        m_prev = m_scr[hh]
        m_new = jnp.maximum(m_prev, jnp.max(s, axis=-1, keepdims=True))
        alpha = jnp.exp(m_prev - m_new)
        p = jnp.exp(s - m_new)
        l_scr[hh] = alpha * l_scr[hh] + jnp.sum(p, axis=-1, keepdims=True)
        acc_scr[hh] = alpha * acc_scr[hh] + jnp.dot(p.astype(BF16), v, preferred_element_type=F32)
        m_scr[hh] = m_new

    @pl.when(j == pl.num_programs(3) - 1)
    def _():
        lane = lax.broadcasted_iota(I32, o_ref.shape[1:], 1)
        o = jnp.where(lane < MLA_V, acc_scr[0] / l_scr[0], acc_scr[1] / l_scr[1])
        o_ref[0] = o.astype(o_ref.dtype)


def _attention(q, k, v):
    b, h, s, _ = q.shape
    tq = min(512, s)
    tk = min(512, s)
    return pl.pallas_call(
        _attn_kernel,
        grid=(b, h // 2, s // tq, s // tk),
        in_specs=[pl.BlockSpec((1, 2, tq, LANES), lambda bi, p, i, j: (bi, p, i, 0)),
                  pl.BlockSpec((1, 2, tk, LANES), lambda bi, p, i, j: (bi, p, j, 0)),
                  pl.BlockSpec((1, tk, LANES), lambda bi, p, i, j: (bi, j, p))],
        out_specs=pl.BlockSpec((1, tq, LANES), lambda bi, p, i, j: (bi, i, p)),
        out_shape=jax.ShapeDtypeStruct((b, s, h * MLA_V), BF16),
        scratch_shapes=[pltpu.VMEM((2, tq, 1), F32), pltpu.VMEM((2, tq, 1), F32),
                        pltpu.VMEM((2, tq, LANES), F32)],
        compiler_params=_params(("parallel", "parallel", "parallel", "arbitrary"), 32),
        name="mla_attention",
    )(q, k, v)


def _merge_kernel(x_ref, og_ref, om_ref, g0_ref, g1_ref, g2_ref, mq_ref, mkv_ref, wbg_ref, wbm_ref,
                  wbe_ref, wo_ref, wq_ref, nq_ref, nk_ref, gf_ref, x1_ref, zn_ref, qq_ref):
    mq = mq_ref[0].astype(F32)
    mkv = mkv_ref[0]
    kv_w = MEM_HEADS * MEM_DH
    outs = []
    for h in range(MEM_HEADS):
        cols = slice(h * MEM_DH, (h + 1) * MEM_DH)
        qh = _rms(mq[:, cols]) * nq_ref[...]
        kh = _rms(mkv[:, cols]) * nk_ref[...]
        vh = mkv[:, kv_w + h * MEM_DH:kv_w + (h + 1) * MEM_DH]
        s = _mm_nt(qh, kh) * (MEM_DH ** -0.5)
        p = jnp.exp(s - jnp.max(s, axis=-1, keepdims=True))
        p = p / jnp.sum(p, axis=-1, keepdims=True)
        outs.append(_mm(p, vh))
    y_mem = _mm(jnp.concatenate(outs, axis=1), wbe_ref[...])
    y_gdn = jnp.dot(og_ref[0], wbg_ref[...], preferred_element_type=F32)
    y_mla = jnp.dot(om_ref[0], wbm_ref[...], preferred_element_type=F32)
    merged = (jax.nn.sigmoid(g0_ref[0].astype(F32)) * y_gdn
              + jax.nn.sigmoid(g1_ref[0].astype(F32)) * y_mla
              + jax.nn.sigmoid(g2_ref[0].astype(F32)) * y_mem)
    x1 = x_ref[0] + _mm(merged, wo_ref[...])
    x1_ref[0] = x1
    zn = _rms(x1) * gf_ref[...]
    zn_ref[0] = zn
    qq_ref[0] = _mm(zn, wq_ref[...])


def _merge(x, o_gdn, o_mla, p3, mkv, wts):
    b, s, d = x.shape
    tm = min(256, s)
    full = lambda shape: pl.BlockSpec(shape, lambda bi, i: tuple(0 for _ in shape))
    tok = lambda w, cb: pl.BlockSpec((1, tm, w), lambda bi, i: (bi, i, cb))
    nq = PEER_HEADS * PEER_DK
    kv_w = MEM_HEADS * MEM_DH
    return pl.pallas_call(
        _merge_kernel,
        grid=(b, s // tm),
        in_specs=[tok(d, 0), tok(d, 0), tok(MLA_HEADS * MLA_V, 0),
                  tok(d, C_GATE // d), tok(d, C_GATE // d + 1), tok(d, C_GATE // d + 2),
                  tok(kv_w, C_MQ // kv_w),
                  pl.BlockSpec((1, MEM_TOKENS, 2 * kv_w), lambda bi, i: (bi, 0, 0)),
                  full((d, d)), full((MLA_HEADS * MLA_V, d)), full((kv_w, d)), full((d, d)),
                  full((d, nq)), full((1, MEM_DH)), full((1, MEM_DH)), full((1, d))],
        out_specs=[tok(d, 0), tok(d, 0), tok(nq, 0)],
        out_shape=[jax.ShapeDtypeStruct((b, s, d), F32), jax.ShapeDtypeStruct((b, s, d), F32),
                   jax.ShapeDtypeStruct((b, s, nq), F32)],
        compiler_params=_params(("parallel", "parallel"), 52),
        name="merge",
    )(x, o_gdn, o_mla, p3, p3, p3, p3, mkv, wts["w_branch_gdn"], wts["w_branch_mla"],
      wts["w_branch_mem"], wts["w_out"], wts["peer_w_q"], wts["mem_qk_norm_q"],
      wts["mem_qk_norm_k"], wts["g_ffn"])


def _top16(cur, payload=None):
    n, t = cur.shape
    rows = lax.broadcasted_iota(I32, (n, t), 0)
    slot = lax.broadcasted_iota(I32, (PEER_TOPK, t), 0)
    vals = jnp.zeros((PEER_TOPK, t), F32)
    picks = jnp.zeros((PEER_TOPK, t), I32)
    for r in range(PEER_TOPK):
        m = jnp.max(cur, axis=0, keepdims=True)
        am = jnp.min(jnp.where(cur == m, rows, n), axis=0, keepdims=True)
        hit = rows == am
        pick = am if payload is None else jnp.max(jnp.where(hit, payload, -1), axis=0, keepdims=True)
        vals = jnp.where(slot == r, m, vals)
        picks = jnp.where(slot == r, pick, picks)
        cur = jnp.where(hit, -jnp.inf, cur)
    return vals, picks


def _peer_topk_kernel(q_ref, ka_ref, kb_ref, eidx_ref, gate_ref):
    half = PEER_DK // 2
    for h in range(PEER_HEADS):
        q1 = q_ref[:, h * PEER_DK:h * PEER_DK + half]
        q2 = q_ref[:, h * PEER_DK + half:(h + 1) * PEER_DK]
        s1 = _mm_nt(ka_ref[h], q1)
        s2 = _mm_nt(kb_ref[h], q2)
        v1, i1 = _top16(s1)
        v2, i2 = _top16(s2)
        cand = jnp.concatenate([v1[a:a + 1, :] + v2 for a in range(PEER_TOPK)], axis=0)
        cidx = jnp.concatenate([i1[a:a + 1, :] * PEER_KEYS + i2 for a in range(PEER_TOPK)], axis=0)
        top_s, eidx = _top16(cand, cidx)
        p = jnp.exp(top_s - top_s[0:1, :])
        rows = slice(h * PEER_TOPK, (h + 1) * PEER_TOPK)
        gate_ref[0, rows, :] = p / jnp.sum(p, axis=0, keepdims=True)
        eidx_ref[0, rows, :] = eidx


def _peer_topk(qq, keys_a, keys_b):
    t, nq = qq.shape
    tp = PEER_TILE
    nt = t // tp
    kspec = pl.BlockSpec((PEER_HEADS, PEER_KEYS, PEER_DK // 2), lambda i: (0, 0, 0))
    return pl.pallas_call(
        _peer_topk_kernel,
        grid=(nt,),
        in_specs=[pl.BlockSpec((tp, nq), lambda i: (i, 0)), kspec, kspec],
        out_specs=[pl.BlockSpec((1, PEER_ROWS, tp), lambda i: (i, 0, 0)),
                   pl.BlockSpec((1, PEER_ROWS, tp), lambda i: (i, 0, 0))],
        out_shape=[jax.ShapeDtypeStruct((nt, PEER_ROWS, tp), I32),
                   jax.ShapeDtypeStruct((nt, PEER_ROWS, tp), F32)],
        compiler_params=_params(("parallel",), 32),
        name="peer_topk",
    )(qq, keys_a, keys_b)


def _peer_mix_kernel(eidx_hbm, uv_hbm, gate_ref, zn_ref, x1_ref, o_ref, idx_smem, buf, sem, isem):
    tp = zn_ref.shape[0]
    half = D_MODEL // 2
    idx_copy = pltpu.make_async_copy(eidx_hbm.at[pl.program_id(0)], idx_smem, isem)
    idx_copy.start()
    idx_copy.wait()
    gate_t = gate_ref[0]

    def issue(t, slot):
        for j in range(PEER_ROWS):
            e = idx_smem[j * tp + t]
            pltpu.make_async_copy(uv_hbm.at[pl.ds(e, 1)], buf.at[slot, pl.ds(j, 1)],
                                  sem.at[slot]).start()

    def wait(slot):
        pltpu.make_async_copy(uv_hbm.at[pl.ds(0, PEER_ROWS)], buf.at[slot], sem.at[slot]).wait()

    for t in range(PEER_BUFS - 1):
        issue(t, t)

    def unpack(words):
        lo = lax.bitcast_convert_type(words << 16, F32)
        hi = lax.bitcast_convert_type(words & jnp.int32(-65536), F32)
        return lo, hi

    def body(t, carry):
        slot = t % PEER_BUFS
        nxt = t + PEER_BUFS - 1

        @pl.when(nxt < tp)
        def _():
            issue(nxt, nxt % PEER_BUFS)

        wait(slot)
        x = buf[slot]
        z = zn_ref[pl.ds(t, 1), :]
        u_lo, u_hi = unpack(x[:, :half])
        prod = u_lo * z[:, :half] + u_hi * z[:, half:]
        part = prod[:, 0:LANES]
        for cch in range(1, half // LANES):
            part = part + prod[:, cch * LANES:(cch + 1) * LANES]
        dots = jnp.sum(part, axis=1, keepdims=True)
        act = 0.5 * dots * (1.0 + lax.erf(dots * (2.0 ** -0.5)))
        g_col = pltpu.roll(gate_t, (tp - t) % tp, 1)[:, 0:1]
        wgt = g_col * act
        v_lo, v_hi = unpack(x[:, half:])
        mix = jnp.concatenate([jnp.sum(wgt * v_lo, axis=0, keepdims=True),
                               jnp.sum(wgt * v_hi, axis=0, keepdims=True)], axis=1)
        o_ref[pl.ds(t, 1), :] = x1_ref[pl.ds(t, 1), :] + mix
        return carry

    lax.fori_loop(0, tp, body, 0)


def _peer_mix(eidx, gate, zn, x1, uv):
    t, d = zn.shape
    tp = PEER_TILE
    nt = t // tp
    return pl.pallas_call(
        _peer_mix_kernel,
        grid=(nt,),
        in_specs=[pl.BlockSpec(memory_space=pl.ANY),
                  pl.BlockSpec(memory_space=pl.ANY),
                  pl.BlockSpec((1, PEER_ROWS, tp), lambda i: (i, 0, 0)),
                  pl.BlockSpec((tp, d), lambda i: (i, 0)),
                  pl.BlockSpec((tp, d), lambda i: (i, 0))],
        out_specs=pl.BlockSpec((tp, d), lambda i: (i, 0)),
        out_shape=jax.ShapeDtypeStruct((t, d), F32),
        scratch_shapes=[pltpu.SMEM((PEER_ROWS * tp,), I32),
                        pltpu.VMEM((PEER_BUFS, PEER_ROWS, d), I32),
                        pltpu.SemaphoreType.DMA((PEER_BUFS,)),
                        pltpu.SemaphoreType.DMA(())],
        compiler_params=_params(("arbitrary",), 32),
        name="peer_mix",
    )(eidx.reshape(nt, PEER_ROWS * tp), uv, gate, zn, x1)


def _pack_pairs(tab):
    bits = lax.bitcast_convert_type(tab.astype(BF16), jnp.uint16).astype(jnp.uint32)
    half = tab.shape[1] // 2
    return lax.bitcast_convert_type(bits[:, :half] | (bits[:, half:] << 16), I32)


def _prepare(w_in, gdn_a_log_fwd, gdn_a_log_bwd, gdn_dt_bias_fwd, gdn_dt_bias_bwd, mla_w_uq, mla_w_ukv,
             mla_qk_norm_q, mla_qk_norm_k, peer_u, peer_v, **others):
    o = 0
    parts = {}
    for name, width in (("qkvz", 4096), ("small", 32), ("cq", MLA_Q_LORA), ("ckv", MLA_KV_LORA),
                        ("kpe", MLA_ROPE), ("mq", MEM_HEADS * MEM_DH), ("gate", 3 * D_MODEL)):
        parts[name] = w_in[:, o:o + width]
        o += width
    wts = dict(others)
    wts["w_main"] = jnp.concatenate(
        [parts["qkvz"], parts["gate"], parts["mq"], parts["cq"], parts["ckv"]], axis=1).astype(BF16)
    wts["w_small"] = jnp.concatenate(
        [parts["small"], parts["kpe"], jnp.zeros((D_MODEL, LANES - 32 - MLA_ROPE), F32)],
        axis=1).astype(BF16)
    zeros16 = jnp.zeros((LANES - 16,), F32)
    wts["a_lane"] = jnp.concatenate(
        [jnp.exp(gdn_a_log_fwd.astype(F32)), jnp.exp(gdn_a_log_bwd.astype(F32)), zeros16]).reshape(1, LANES)
    wts["dtb_lane"] = jnp.concatenate(
        [gdn_dt_bias_fwd.astype(F32), gdn_dt_bias_bwd.astype(F32), zeros16]).reshape(1, LANES)
    pad_q = jnp.zeros((MLA_Q_LORA, MLA_HEADS, LANES - MLA_QK), F32)
    wts["w_uq"] = jnp.concatenate(
        [mla_w_uq.reshape(MLA_Q_LORA, MLA_HEADS, MLA_QK), pad_q], axis=2
    ).reshape(MLA_Q_LORA, MLA_HEADS * LANES).astype(BF16)
    ukv = mla_w_ukv.reshape(MLA_KV_LORA, MLA_HEADS, MLA_NOPE + MLA_V)
    pad_k = jnp.zeros((MLA_KV_LORA, MLA_HEADS, LANES - MLA_NOPE), F32)
    wts["w_uk"] = jnp.concatenate([ukv[:, :, :MLA_NOPE], pad_k], axis=2).reshape(
        MLA_KV_LORA, MLA_HEADS * LANES).astype(BF16)
    wts["w_uv"] = ukv[:, :, MLA_NOPE:].reshape(MLA_KV_LORA, MLA_HEADS * MLA_V).astype(BF16)
    pad_n = jnp.zeros((LANES - MLA_QK,), F32)
    wts["qk_norm_q"] = jnp.concatenate([mla_qk_norm_q, pad_n]).reshape(1, LANES)
    wts["qk_norm_k"] = jnp.concatenate([mla_qk_norm_k, pad_n]).reshape(1, LANES)
    for name in ("mla_q_norm", "mla_kv_norm", "mem_qk_norm_q", "mem_qk_norm_k", "g_ffn"):
        wts[name] = wts[name].reshape(1, -1)
    for name in ("w_branch_gdn", "w_branch_mla", "w_branch_mem", "w_out", "peer_w_q", "mem_w_kv",
                 "peer_keys_a", "peer_keys_b"):
        wts[name] = wts[name].astype(BF16)
    wts["peer_uv"] = jnp.concatenate([_pack_pairs(peer_u), _pack_pairs(peer_v)], axis=1)
    return wts


def _rope_tables(s):
    half = MLA_ROPE // 2
    inv = ROPE_THETA ** (-jnp.arange(half, dtype=F32) / half)
    ang = jnp.arange(s).astype(F32)[:, None] * inv[None, :]
    cos = jnp.cos(ang)
    sin = jnp.sin(ang)
    ones = jnp.ones((s, MLA_NOPE), F32)
    pad = jnp.zeros((s, LANES - MLA_QK), F32)
    zeros_n = jnp.zeros((s, MLA_NOPE), F32)
    zeros_h = jnp.zeros((s, half), F32)
    cos_t = jnp.concatenate([ones, cos, cos, pad], axis=1)
    sin_a = jnp.concatenate([zeros_n, zeros_h, sin, pad], axis=1)
    sin_b = jnp.concatenate([zeros_n, -sin, zeros_h, pad], axis=1)
    return cos_t, sin_a, sin_b


def _layer(x, mem, wts):
    b, s, d = x.shape
    t = b * s
    p_main, p_small = _norm_proj(x.reshape(t, d), wts["g_mix"], wts["w_main"], wts["w_small"], 5, BF16)
    p3 = p_main.reshape(b, s, N_MAIN)
    ps3 = p_small.reshape(b, s, LANES)
    qkv = _gdn_conv(p3, wts["gdn_conv_w"])
    o_fwd = _gdn_scan(qkv, ps3, wts["a_lane"], wts["dtb_lane"], False)
    o_gdn = _gdn_scan(qkv, ps3, wts["a_lane"], wts["dtb_lane"], True, o_fwd, p3, wts["gdn_out_norm"])
    q_m, k_m, v_m = _mla_prep(p3, ps3, _rope_tables(s), wts)
    o_mla = _attention(q_m, k_m, v_m)
    (mkv,) = _norm_proj(mem.reshape(b * MEM_TOKENS, d), wts["mem_norm"], wts["mem_w_kv"], None, 1, F32)
    mkv = mkv.reshape(b, MEM_TOKENS, 2 * MEM_HEADS * MEM_DH)
    x1, zn, qq = _merge(x, o_gdn, o_mla, p3, mkv, wts)
    eidx, gate = _peer_topk(qq.reshape(t, -1), wts["peer_keys_a"], wts["peer_keys_b"])
    y = _peer_mix(eidx, gate, zn.reshape(t, d), x1.reshape(t, d), wts["peer_uv"])
    return y.reshape(b, s, d)


def kernel(x_prompt, x_sample, mem_prompt, mem_sample, g_mix, w_in, gdn_conv_w, gdn_a_log_fwd,
           gdn_a_log_bwd, gdn_dt_bias_fwd, gdn_dt_bias_bwd, gdn_out_norm, mla_q_norm, mla_w_uq,
           mla_kv_norm, mla_w_ukv, mla_qk_norm_q, mla_qk_norm_k, mem_norm, mem_w_kv, mem_qk_norm_q,
           mem_qk_norm_k, w_branch_gdn, w_branch_mla, w_branch_mem, w_out, g_ffn, peer_w_q,
           peer_keys_a, peer_keys_b, peer_u, peer_v):
    wts = _prepare(
        w_in, gdn_a_log_fwd, gdn_a_log_bwd, gdn_dt_bias_fwd, gdn_dt_bias_bwd, mla_w_uq, mla_w_ukv,
        mla_qk_norm_q, mla_qk_norm_k, peer_u, peer_v,
        g_mix=g_mix, gdn_conv_w=gdn_conv_w, gdn_out_norm=gdn_out_norm, mla_q_norm=mla_q_norm,
        mla_kv_norm=mla_kv_norm, mem_norm=mem_norm, mem_w_kv=mem_w_kv, mem_qk_norm_q=mem_qk_norm_q,
        mem_qk_norm_k=mem_qk_norm_k, w_branch_gdn=w_branch_gdn, w_branch_mla=w_branch_mla,
        w_branch_mem=w_branch_mem, w_out=w_out, g_ffn=g_ffn, peer_w_q=peer_w_q,
        peer_keys_a=peer_keys_a, peer_keys_b=peer_keys_b)
    return (_layer(x_prompt, mem_prompt, wts), _layer(x_sample, mem_sample, wts))
```

```python
import functools

import jax
import jax.numpy as jnp
from jax import lax
from jax.experimental import pallas as pl
from jax.experimental.pallas import tpu as pltpu

F32 = jnp.float32
BF16 = jnp.bfloat16
I32 = jnp.int32

D_MODEL = 1024
GDN_HEADS = 8
GDN_DK = 128
GDN_DV = 128
GDN_CONV = 5
GDN_CHUNK = 64
MLA_HEADS = 8
MLA_NOPE = 64
MLA_ROPE = 32
MLA_QK = MLA_NOPE + MLA_ROPE
MLA_V = 64
MLA_Q_LORA = 384
MLA_KV_LORA = 256
ROPE_THETA = 10000.0
MEM_TOKENS = 256
MEM_HEADS = 4
MEM_DH = 128
PEER_HEADS = 8
PEER_DK = 256
PEER_KEYS = 128
PEER_TOPK = 16
NORM_EPS = 1e-6
LOG2E = 1.4426950408889634

LANES = 128
MIB = 1024 * 1024

C_Q, C_K, C_V, C_Z = 0, 1024, 2048, 3072
C_GATE = 4096
C_MQ = 7168
C_LAT = 7680
N_MAIN = 8320
L_AF, L_AB, L_BF, L_BB, L_KPE = 0, 8, 16, 24, 32

PEER_TILE = 128
PEER_ROWS = PEER_HEADS * PEER_TOPK
PEER_BUFS = 4


def _params(sem, vmem_mib):
    return pltpu.CompilerParams(dimension_semantics=sem, vmem_limit_bytes=vmem_mib * MIB)


def _mm(a, b):
    return jnp.dot(a.astype(BF16), b.astype(BF16), preferred_element_type=F32)


def _mm_nt(a, b):
    return lax.dot_general(a.astype(BF16), b.astype(BF16), (((1,), (1,)), ((), ())),
                           preferred_element_type=F32)


def _mm_tn(a, b):
    return lax.dot_general(a.astype(BF16), b.astype(BF16), (((0,), (0,)), ((), ())),
                           preferred_element_type=F32)


def _rms(x, n=None):
    n = x.shape[-1] if n is None else n
    return x * lax.rsqrt(jnp.sum(x * x, axis=-1, keepdims=True) * (1.0 / n) + NORM_EPS)


def _norm_proj_kernel(has_small, x_ref, g_ref, w_ref, *rest):
    if has_small:
        ws_ref, o_ref, os_ref, h_scr = rest
    else:
        o_ref, h_scr = rest

    @pl.when(pl.program_id(1) == 0)
    def _():
        h = _rms(x_ref[...]) * g_ref[...]
        h_scr[...] = h.astype(BF16)
        if has_small:
            os_ref[...] = jnp.dot(h_scr[...], ws_ref[...], preferred_element_type=F32)

    o_ref[...] = jnp.dot(h_scr[...], w_ref[...], preferred_element_type=F32).astype(o_ref.dtype)


def _norm_proj(x2d, g, w, w_small, n_col_blocks, out_dtype):
    t, d = x2d.shape
    n = w.shape[1]
    tm = min(1024, t)
    tn = n // n_col_blocks
    in_specs = [pl.BlockSpec((tm, d), lambda i, j: (i, 0)),
                pl.BlockSpec((1, d), lambda i, j: (0, 0)),
                pl.BlockSpec((d, tn), lambda i, j: (0, j))]
    out_shape = [jax.ShapeDtypeStruct((t, n), out_dtype)]
    out_specs = [pl.BlockSpec((tm, tn), lambda i, j: (i, j))]
    args = [x2d, g.reshape(1, d), w]
    if w_small is not None:
        in_specs.append(pl.BlockSpec((d, LANES), lambda i, j: (0, 0)))
        out_shape.append(jax.ShapeDtypeStruct((t, LANES), F32))
        out_specs.append(pl.BlockSpec((tm, LANES), lambda i, j: (i, 0)))
        args.append(w_small)
    return pl.pallas_call(
        functools.partial(_norm_proj_kernel, w_small is not None),
        grid=(t // tm, n_col_blocks),
        in_specs=in_specs, out_specs=out_specs, out_shape=out_shape,
        scratch_shapes=[pltpu.VMEM((tm, d), BF16)],
        compiler_params=_params(("parallel", "arbitrary"), 48),
        name="norm_proj",
    )(*args)


def _gdn_conv_kernel(ts, x_ref, w_ref, o_ref):
    s = x_ref.shape[1]
    n_chunks = s // ts
    halo = 16
    n_ext = ts + 2 * halo
    is_qk = pl.program_id(1) < 2 * GDN_HEADS
    w = w_ref[...]

    def body(i, carry):
        t0 = pl.multiple_of(i * ts, ts)
        main = x_ref[0, pl.ds(t0, ts), :].astype(F32)
        p0 = pl.multiple_of(jnp.maximum(t0 - halo, 0), halo)
        n0 = pl.multiple_of(jnp.minimum(t0 + ts, s - halo), halo)
        prev = jnp.where(i > 0, x_ref[0, pl.ds(p0, halo), :].astype(F32), 0.0)
        nxt = jnp.where(i < n_chunks - 1, x_ref[0, pl.ds(n0, halo), :].astype(F32), 0.0)
        ext = jnp.concatenate([prev, main, nxt], axis=0)
        acc = w[2:3, :] * ext
        for j in (0, 1, 3, 4):
            acc = acc + w[j:j + 1, :] * pltpu.roll(ext, (2 - j) % n_ext, 0)
        y = acc[halo:halo + ts]
        y = y * jax.nn.sigmoid(y)
        y_n = y * lax.rsqrt(jnp.sum(y * y, axis=-1, keepdims=True) + NORM_EPS)
        o_ref[0, 0, pl.ds(t0, ts), :] = jnp.where(is_qk, y_n, y).astype(o_ref.dtype)
        return carry

    lax.fori_loop(0, n_chunks, body, 0)


def _gdn_conv(p3, conv_w):
    b, s, _ = p3.shape
    ts = min(512, s)
    n_blk = 3 * GDN_HEADS
    return pl.pallas_call(
        functools.partial(_gdn_conv_kernel, ts),
        grid=(b, n_blk),
        in_specs=[pl.BlockSpec((1, s, LANES), lambda bi, c: (bi, 0, c)),
                  pl.BlockSpec((GDN_CONV, LANES), lambda bi, c: (0, c))],
        out_specs=pl.BlockSpec((1, 1, s, LANES), lambda bi, c: (bi, c, 0, 0)),
        out_shape=jax.ShapeDtypeStruct((b, n_blk, s, LANES), BF16),
        compiler_params=_params(("parallel", "parallel"), 40),
        name="gdn_conv",
    )(p3, conv_w)


def _gdn_scan_kernel(reverse, hb, ts, *refs):
    if reverse:
        q_ref, k_ref, v_ref, ps_ref, al_ref, dtb_ref, of_ref, z_ref, gn_ref, o_ref, s_scr = refs
    else:
        q_ref, k_ref, v_ref, ps_ref, al_ref, dtb_ref, o_ref, s_scr = refs
    c = GDN_CHUNK
    n_chunks = ts // c

    @pl.when(pl.program_id(2) == 0)
    def _():
        s_scr[...] = jnp.zeros_like(s_scr)

    ps = ps_ref[0]
    g = -al_ref[...] * jax.nn.softplus(ps + dtb_ref[...])
    beta = jax.nn.sigmoid(ps)
    rowid = lax.broadcasted_iota(I32, (ts, LANES), 0) & (c - 1)
    gc = g
    for kk in (1, 2, 4, 8, 16, 32):
        if reverse:
            gc = gc + jnp.where(rowid < c - kk, pltpu.roll(gc, ts - kk, 0), 0.0)
        else:
            gc = gc + jnp.where(rowid >= kk, pltpu.roll(gc, kk, 0), 0.0)
    shift = (LANES - pl.program_id(1) * hb) % LANES
    gc = pltpu.roll(gc, shift, 1)
    beta = pltpu.roll(beta, shift, 1)
    eg = jnp.exp(gc)
    gc_t = [gc[p * LANES:(p + 1) * LANES, :].T for p in range(ts // LANES)]

    r = lax.broadcasted_iota(I32, (c, c), 0)
    cc = lax.broadcasted_iota(I32, (c, c), 1)
    if reverse:
        r, cc = cc, r
    incl = r >= cc
    strict = r > cc
    eye = jnp.where(r == cc, 1.0, 0.0)
    level_f = [jnp.where((((r >> s) & 1) == 1) & ((cc >> s) == (r >> s) - 1), 1.0, 0.0)
               for s in range(6)]

    lane_g = L_AB if reverse else L_AF
    lane_b = L_BB if reverse else L_BF
    scale = GDN_DK ** -0.5
    order = list(range(n_chunks - 1, -1, -1)) if reverse else list(range(n_chunks))
    units = [(ci, hh) for ci in order for hh in range(hb)]

    kb16, k16, q16, rhs, dec, kd16, a2q, gl = {}, {}, {}, {}, {}, {}, {}, {}
    for un in units:
        ci, hh = un
        r0 = ci * c
        q = q_ref[0, hh, r0:r0 + c, :].astype(F32) * scale
        k = k_ref[0, hh, r0:r0 + c, :].astype(F32)
        v = v_ref[0, hh, r0:r0 + c, :].astype(F32)
        la = lane_g + hh
        lb = lane_b + hh
        gc_c = gc[r0:r0 + c, la:la + 1]
        eg_c = eg[r0:r0 + c, la:la + 1]
        b_c = beta[r0:r0 + c, lb:lb + 1]
        gc_r = gc_t[r0 // LANES][la:la + 1, r0 % LANES:r0 % LANES + c]
        r_last = r0 if reverse else r0 + c - 1
        gc_l = gc[r_last:r_last + 1, la:la + 1]
        dec[un] = jnp.exp(jnp.where(incl, gc_c - gc_r, -1e30))
        kb = k * b_c
        kb16[un] = jnp.concatenate([kb, q], axis=0).astype(BF16)
        k16[un] = k.astype(BF16)
        rhs[un] = jnp.concatenate([v * b_c, kb * eg_c], axis=1).astype(BF16)
        a2q[un] = (q * eg_c).astype(BF16)
        kd16[un] = (k * jnp.exp(gc_l - gc_c)).astype(BF16)
        gl[un] = jnp.exp(gc_l)
    kq = {un: lax.dot_general(kb16[un], k16[un], (((1,), (1,)), ((), ())),
                              preferred_element_type=F32) for un in units}
    lm = {un: jnp.where(strict, kq[un][:c] * dec[un], 0.0) for un in units}
    intra = {un: jnp.where(incl, kq[un][c:] * dec[un], 0.0).astype(BF16) for un in units}
    dm = {un: eye - lm[un] * level_f[0] for un in units}
    for s in range(1, 6):
        cd = {un: _mm(lm[un] * level_f[s], dm[un]) for un in units}
        dm = {un: dm[un] - _mm(dm[un], cd[un]) for un in units}
    sol = {un: _mm(dm[un], rhs[un]) for un in units}

    st = [s_scr[hh] for hh in range(hb)]
    for ci in order:
        r0 = ci * c
        heads = [(ci, hh) for hh in range(hb)]
        a2 = [jnp.concatenate([sol[un][:, GDN_DV:].astype(BF16), a2q[un]], axis=0) for un in heads]
        ws_qs = [jnp.dot(a2[hh], st[hh].astype(BF16), preferred_element_type=F32) for hh in range(hb)]
        v_new = [sol[un][:, :GDN_DV] - ws_qs[un[1]][:c] for un in heads]
        v16 = [vn.astype(BF16) for vn in v_new]
        o = [ws_qs[hh][c:] + jnp.dot(intra[(ci, hh)], v16[hh], preferred_element_type=F32)
             for hh in range(hb)]
        st = [st[hh] * gl[(ci, hh)]
              + lax.dot_general(kd16[(ci, hh)], v16[hh], (((0,), (0,)), ((), ())),
                                preferred_element_type=F32) for hh in range(hb)]
        for hh in range(hb):
            cols = slice(hh * GDN_DV, (hh + 1) * GDN_DV)
            if reverse:
                tot = o[hh] + of_ref[0, r0:r0 + c, cols]
                zz = z_ref[0, r0:r0 + c, cols].astype(F32)
                y = _rms(tot) * gn_ref[...] * (zz * jax.nn.sigmoid(zz))
                o_ref[0, r0:r0 + c, cols] = y.astype(o_ref.dtype)
            else:
                o_ref[0, r0:r0 + c, cols] = o[hh]
    for hh in range(hb):
        s_scr[hh] = st[hh]


def _gdn_scan(qkv, ps3, a_lane, dtb_lane, reverse, o_fwd=None, p3=None, out_norm=None):
    b, _, s, _ = qkv.shape
    hb = 4
    ts = 256
    nt = s // ts
    n_hblk = GDN_HEADS // hb
    tile = (lambda i: nt - 1 - i) if reverse else (lambda i: i)
    in_specs = [
        pl.BlockSpec((1, hb, ts, LANES), lambda bi, h, i: (bi, h, tile(i), 0)),
        pl.BlockSpec((1, hb, ts, LANES), lambda bi, h, i: (bi, n_hblk + h, tile(i), 0)),
        pl.BlockSpec((1, hb, ts, LANES), lambda bi, h, i: (bi, 2 * n_hblk + h, tile(i), 0)),
        pl.BlockSpec((1, ts, LANES), lambda bi, h, i: (bi, tile(i), 0)),
        pl.BlockSpec((1, LANES), lambda bi, h, i: (0, 0)),
        pl.BlockSpec((1, LANES), lambda bi, h, i: (0, 0)),
    ]
    args = [qkv, qkv, qkv, ps3, a_lane, dtb_lane]
    wide = hb * GDN_DV
    if reverse:
        in_specs += [
            pl.BlockSpec((1, ts, wide), lambda bi, h, i: (bi, tile(i), h)),
            pl.BlockSpec((1, ts, wide), lambda bi, h, i: (bi, tile(i), C_Z // wide + h)),
            pl.BlockSpec((1, GDN_DV), lambda bi, h, i: (0, 0)),
        ]
        args += [o_fwd, p3, out_norm.reshape(1, GDN_DV)]
    out_dtype = BF16 if reverse else F32
    return pl.pallas_call(
        functools.partial(_gdn_scan_kernel, reverse, hb, ts),
        grid=(b, n_hblk, nt),
        in_specs=in_specs,
        out_specs=pl.BlockSpec((1, ts, wide), lambda bi, h, i: (bi, tile(i), h)),
        out_shape=jax.ShapeDtypeStruct((b, s, GDN_HEADS * GDN_DV), out_dtype),
        scratch_shapes=[pltpu.VMEM((hb, GDN_DK, GDN_DV), F32)],
        compiler_params=_params(("parallel", "parallel", "arbitrary"), 32),
        name="gdn_scan_bwd" if reverse else "gdn_scan_fwd",
    )(*args)


def _mla_prep_kernel(lat_ref, ps_ref, cos_ref, sa_ref, sb_ref, gq_ref, gkv_ref, wuq_ref, wuk_ref,
                     wuv_ref, nq_ref, nk_ref, q_ref, k_ref, v_ref):
    lat = lat_ref[0].astype(F32)
    cq = _rms(lat[:, :MLA_Q_LORA]) * gq_ref[...]
    ckv = _rms(lat[:, MLA_Q_LORA:]) * gkv_ref[...]
    q_all = _mm(cq, wuq_ref[...])
    k_all = _mm(ckv, wuk_ref[...])
    v_ref[0] = _mm(ckv, wuv_ref[...]).astype(v_ref.dtype)
    ps = ps_ref[0]
    lane = lax.broadcasted_iota(I32, ps.shape, 1)
    in_rope = (lane >= MLA_NOPE) & (lane < MLA_QK)
    kpe = jnp.where(in_rope, pltpu.roll(ps, MLA_NOPE - L_KPE, 1), 0.0)
    cos = cos_ref[...]
    sin_a = sa_ref[...]
    sin_b = sb_ref[...]
    half = MLA_ROPE // 2

    def norm_rope(x, gain):
        y = _rms(x, MLA_QK) * gain
        return y * cos + pltpu.roll(y, half, 1) * sin_a + pltpu.roll(y, LANES - half, 1) * sin_b

    for h in range(MLA_HEADS):
        cols = slice(h * LANES, (h + 1) * LANES)
        q_ref[0, h] = (norm_rope(q_all[:, cols], nq_ref[...]) * (MLA_QK ** -0.5 * LOG2E)).astype(q_ref.dtype)
        k_ref[0, h] = norm_rope(k_all[:, cols] + kpe, nk_ref[...]).astype(k_ref.dtype)


def _mla_prep(p3, ps3, rope_tabs, wts):
    b, s, _ = p3.shape
    tm = min(512, s)
    lat_w = MLA_Q_LORA + MLA_KV_LORA
    full = lambda shape: pl.BlockSpec(shape, lambda bi, i: tuple(0 for _ in shape))
    tab = pl.BlockSpec((tm, LANES), lambda bi, i: (i, 0))
    hw = MLA_HEADS * LANES
    return pl.pallas_call(
        _mla_prep_kernel,
        grid=(b, s // tm),
        in_specs=[pl.BlockSpec((1, tm, lat_w), lambda bi, i: (bi, i, C_LAT // lat_w)),
                  pl.BlockSpec((1, tm, LANES), lambda bi, i: (bi, i, 0)),
                  tab, tab, tab,
                  full((1, MLA_Q_LORA)), full((1, MLA_KV_LORA)),
                  full((MLA_Q_LORA, hw)), full((MLA_KV_LORA, hw)),
                  full((MLA_KV_LORA, MLA_HEADS * MLA_V)),
                  full((1, LANES)), full((1, LANES))],
        out_specs=[pl.BlockSpec((1, MLA_HEADS, tm, LANES), lambda bi, i: (bi, 0, i, 0)),
                   pl.BlockSpec((1, MLA_HEADS, tm, LANES), lambda bi, i: (bi, 0, i, 0)),
                   pl.BlockSpec((1, tm, MLA_HEADS * MLA_V), lambda bi, i: (bi, i, 0))],
        out_shape=[jax.ShapeDtypeStruct((b, MLA_HEADS, s, LANES), BF16),
                   jax.ShapeDtypeStruct((b, MLA_HEADS, s, LANES), BF16),
                   jax.ShapeDtypeStruct((b, s, MLA_HEADS * MLA_V), BF16)],
        compiler_params=_params(("parallel", "parallel"), 40),
        name="mla_prep",
    )(p3, ps3, *rope_tabs, wts["mla_q_norm"], wts["mla_kv_norm"], wts["w_uq"], wts["w_uk"],
      wts["w_uv"], wts["qk_norm_q"], wts["qk_norm_k"])


def _attn_kernel(tk_sub, q_ref, k_ref, v_ref, o_ref, m_scr, l_scr, acc_scr):
    j = pl.program_id(3)

    @pl.when(j == 0)
    def _():
        m_scr[...] = jnp.full_like(m_scr, -jnp.inf)
        l_scr[...] = jnp.zeros_like(l_scr)
        acc_scr[...] = jnp.zeros_like(acc_scr)

    n_sub = k_ref.shape[2] // tk_sub
    n_lt = tk_sub // LANES
    m = [m_scr[hh] for hh in range(2)]
    l = [l_scr[hh] for hh in range(2)]
    acc = [acc_scr[hh] for hh in range(2)]
    for sb in range(n_sub):
        rows = slice(sb * tk_sub, (sb + 1) * tk_sub)
        v = v_ref[0, rows, :]
        for hh in range(2):
            s = lax.dot_general(q_ref[0, hh], k_ref[0, hh, rows, :], (((1,), (1,)), ((), ())),
                                preferred_element_type=F32)
            tiles = [s[:, t * LANES:(t + 1) * LANES] for t in range(n_lt)]
            mx = tiles[0]
            for t in tiles[1:]:
                mx = jnp.maximum(mx, t)
            m_new = jnp.maximum(m[hh], jnp.max(mx, axis=-1, keepdims=True))
            alpha = jnp.exp2(m[hh] - m_new)
            ps = [jnp.exp2(t - m_new) for t in tiles]
            psum = ps[0]
            for t in ps[1:]:
                psum = psum + t
            l[hh] = alpha * l[hh] + psum
            p = jnp.concatenate(ps, axis=1).astype(BF16)
            acc[hh] = alpha * acc[hh] + jnp.dot(p, v, preferred_element_type=F32)
            m[hh] = m_new
    for hh in range(2):
        m_scr[hh] = m[hh]
        l_scr[hh] = l[hh]
        acc_scr[hh] = acc[hh]

    @pl.when(j == pl.num_programs(3) - 1)
    def _():
        lane = lax.broadcasted_iota(I32, o_ref.shape[1:], 1)
        o0 = acc[0] / jnp.sum(l[0], axis=-1, keepdims=True)
        o1 = acc[1] / jnp.sum(l[1], axis=-1, keepdims=True)
        o_ref[0] = jnp.where(lane < MLA_V, o0, o1).astype(o_ref.dtype)


def _attention(q, k, v):
    b, h, s, _ = q.shape
    tq = min(1024, s)
    tk = min(2048, s)
    tk_sub = min(512, tk)
    return pl.pallas_call(
        functools.partial(_attn_kernel, tk_sub),
        grid=(b, h // 2, s // tq, s // tk),
        in_specs=[pl.BlockSpec((1, 2, tq, LANES), lambda bi, p, i, j: (bi, p, i, 0)),
                  pl.BlockSpec((1, 2, tk, LANES), lambda bi, p, i, j: (bi, p, j, 0)),
                  pl.BlockSpec((1, tk, LANES), lambda bi, p, i, j: (bi, j, p))],
        out_specs=pl.BlockSpec((1, tq, LANES), lambda bi, p, i, j: (bi, i, p)),
        out_shape=jax.ShapeDtypeStruct((b, s, h * MLA_V), BF16),
        scratch_shapes=[pltpu.VMEM((2, tq, LANES), F32), pltpu.VMEM((2, tq, LANES), F32),
                        pltpu.VMEM((2, tq, LANES), F32)],
        compiler_params=_params(("parallel", "parallel", "parallel", "arbitrary"), 40),
        name="mla_attention",
    )(q, k, v)


def _merge_kernel(x_ref, og_ref, om_ref, g0_ref, g1_ref, g2_ref, mq_ref, mkv_ref, wbg_ref, wbm_ref,
                  wbe_ref, wo_ref, wq_ref, nq_ref, nk_ref, gf_ref, x1_ref, zn_ref, qq_ref):
    mq = mq_ref[0].astype(F32)
    mkv = mkv_ref[0]
    kv_w = MEM_HEADS * MEM_DH
    outs = []
    for h in range(MEM_HEADS):
        cols = slice(h * MEM_DH, (h + 1) * MEM_DH)
        qh = _rms(mq[:, cols]) * nq_ref[...]
        kh = _rms(mkv[:, cols]) * nk_ref[...]
        vh = mkv[:, kv_w + h * MEM_DH:kv_w + (h + 1) * MEM_DH]
        s = _mm_nt(qh, kh) * (MEM_DH ** -0.5)
        p = jnp.exp(s - jnp.max(s, axis=-1, keepdims=True))
        p = p / jnp.sum(p, axis=-1, keepdims=True)
        outs.append(_mm(p, vh))
    y_mem = _mm(jnp.concatenate(outs, axis=1), wbe_ref[...])
    y_gdn = jnp.dot(og_ref[0], wbg_ref[...], preferred_element_type=F32)
    y_mla = jnp.dot(om_ref[0], wbm_ref[...], preferred_element_type=F32)
    merged = (jax.nn.sigmoid(g0_ref[0].astype(F32)) * y_gdn
              + jax.nn.sigmoid(g1_ref[0].astype(F32)) * y_mla
              + jax.nn.sigmoid(g2_ref[0].astype(F32)) * y_mem)
    x1 = x_ref[0] + _mm(merged, wo_ref[...])
    x1_ref[0] = x1
    zn = _rms(x1) * gf_ref[...]
    zn_ref[0] = zn
    qq_ref[0] = _mm(zn, wq_ref[...])


def _merge(x, o_gdn, o_mla, p3, mkv, wts):
    b, s, d = x.shape
    tm = min(256, s)
    full = lambda shape: pl.BlockSpec(shape, lambda bi, i: tuple(0 for _ in shape))
    tok = lambda w, cb: pl.BlockSpec((1, tm, w), lambda bi, i: (bi, i, cb))
    nq = PEER_HEADS * PEER_DK
    kv_w = MEM_HEADS * MEM_DH
    return pl.pallas_call(
        _merge_kernel,
        grid=(b, s // tm),
        in_specs=[tok(d, 0), tok(d, 0), tok(MLA_HEADS * MLA_V, 0),
                  tok(d, C_GATE // d), tok(d, C_GATE // d + 1), tok(d, C_GATE // d + 2),
                  tok(kv_w, C_MQ // kv_w),
                  pl.BlockSpec((1, MEM_TOKENS, 2 * kv_w), lambda bi, i: (bi, 0, 0)),
                  full((d, d)), full((MLA_HEADS * MLA_V, d)), full((kv_w, d)), full((d, d)),
                  full((d, nq)), full((1, MEM_DH)), full((1, MEM_DH)), full((1, d))],
        out_specs=[tok(d, 0), tok(d, 0), tok(nq, 0)],
        out_shape=[jax.ShapeDtypeStruct((b, s, d), F32), jax.ShapeDtypeStruct((b, s, d), F32),
                   jax.ShapeDtypeStruct((b, s, nq), F32)],
        compiler_params=_params(("parallel", "parallel"), 52),
        name="merge",
    )(x, o_gdn, o_mla, p3, p3, p3, p3, mkv, wts["w_branch_gdn"], wts["w_branch_mla"],
      wts["w_branch_mem"], wts["w_out"], wts["peer_w_q"], wts["mem_qk_norm_q"],
      wts["mem_qk_norm_k"], wts["g_ffn"])


def _top16(cur, payload=None):
    n, t = cur.shape
    rows = lax.broadcasted_iota(I32, (n, t), 0)
    slot = lax.broadcasted_iota(I32, (PEER_TOPK, t), 0)
    vals = jnp.zeros((PEER_TOPK, t), F32)
    picks = jnp.zeros((PEER_TOPK, t), I32)
    for r in range(PEER_TOPK):
        m = jnp.max(cur, axis=0, keepdims=True)
        am = jnp.min(jnp.where(cur == m, rows, n), axis=0, keepdims=True)
        hit = rows == am
        pick = am if payload is None else jnp.max(jnp.where(hit, payload, -1), axis=0, keepdims=True)
        vals = jnp.where(slot == r, m, vals)
        picks = jnp.where(slot == r, pick, picks)
        cur = jnp.where(hit, -jnp.inf, cur)
    return vals, picks


def _peer_topk_kernel(q_ref, ka_ref, kb_ref, eidx_ref, gate_ref):
    half = PEER_DK // 2
    for h in range(PEER_HEADS):
        q1 = q_ref[:, h * PEER_DK:h * PEER_DK + half]
        q2 = q_ref[:, h * PEER_DK + half:(h + 1) * PEER_DK]
        s1 = _mm_nt(ka_ref[h], q1)
        s2 = _mm_nt(kb_ref[h], q2)
        v1, i1 = _top16(s1)
        v2, i2 = _top16(s2)
        cand = jnp.concatenate([v1[a:a + 1, :] + v2 for a in range(PEER_TOPK)], axis=0)
        cidx = jnp.concatenate([i1[a:a + 1, :] * PEER_KEYS + i2 for a in range(PEER_TOPK)], axis=0)
        top_s, eidx = _top16(cand, cidx)
        p = jnp.exp(top_s - top_s[0:1, :])
        rows = slice(h * PEER_TOPK, (h + 1) * PEER_TOPK)
        gate_ref[0, rows, :] = p / jnp.sum(p, axis=0, keepdims=True)
        eidx_ref[0, rows, :] = eidx


def _peer_topk(qq, keys_a, keys_b):
    t, nq = qq.shape
    tp = PEER_TILE
    nt = t // tp
    kspec = pl.BlockSpec((PEER_HEADS, PEER_KEYS, PEER_DK // 2), lambda i: (0, 0, 0))
    return pl.pallas_call(
        _peer_topk_kernel,
        grid=(nt,),
        in_specs=[pl.BlockSpec((tp, nq), lambda i: (i, 0)), kspec, kspec],
        out_specs=[pl.BlockSpec((1, PEER_ROWS, tp), lambda i: (i, 0, 0)),
                   pl.BlockSpec((1, PEER_ROWS, tp), lambda i: (i, 0, 0))],
        out_shape=[jax.ShapeDtypeStruct((nt, PEER_ROWS, tp), I32),
                   jax.ShapeDtypeStruct((nt, PEER_ROWS, tp), F32)],
        compiler_params=_params(("parallel",), 32),
        name="peer_topk",
    )(qq, keys_a, keys_b)


def _peer_mix_kernel(eidx_hbm, uv_hbm, gate_ref, zn_ref, x1_ref, o_ref, idx_smem, buf, sem, isem):
    tp = zn_ref.shape[0]
    half = D_MODEL // 2
    idx_copy = pltpu.make_async_copy(eidx_hbm.at[pl.program_id(0)], idx_smem, isem)
    idx_copy.start()
    idx_copy.wait()
    gate_t = gate_ref[0]

    def issue(t, slot):
        for j in range(PEER_ROWS):
            e = idx_smem[j * tp + t]
            pltpu.make_async_copy(uv_hbm.at[pl.ds(e, 1)], buf.at[slot, pl.ds(j, 1)],
                                  sem.at[slot]).start()

    def wait(slot):
        pltpu.make_async_copy(uv_hbm.at[pl.ds(0, PEER_ROWS)], buf.at[slot], sem.at[slot]).wait()

    for t in range(PEER_BUFS - 1):
        issue(t, t)

    def unpack(words):
        lo = lax.bitcast_convert_type(words << 16, F32)
        hi = lax.bitcast_convert_type(words & jnp.int32(-65536), F32)
        return lo, hi

    def body(t, carry):
        slot = t % PEER_BUFS
        nxt = t + PEER_BUFS - 1

        @pl.when(nxt < tp)
        def _():
            issue(nxt, nxt % PEER_BUFS)

        wait(slot)
        x = buf[slot]
        z = zn_ref[pl.ds(t, 1), :]
        u_lo, u_hi = unpack(x[:, :half])
        prod = u_lo * z[:, :half] + u_hi * z[:, half:]
        part = prod[:, 0:LANES]
        for cch in range(1, half // LANES):
            part = part + prod[:, cch * LANES:(cch + 1) * LANES]
        dots = jnp.sum(part, axis=1, keepdims=True)
        act = 0.5 * dots * (1.0 + lax.erf(dots * (2.0 ** -0.5)))
        g_col = pltpu.roll(gate_t, (tp - t) % tp, 1)[:, 0:1]
        wgt = g_col * act
        v_lo, v_hi = unpack(x[:, half:])
        mix = jnp.concatenate([jnp.sum(wgt * v_lo, axis=0, keepdims=True),
                               jnp.sum(wgt * v_hi, axis=0, keepdims=True)], axis=1)
        o_ref[pl.ds(t, 1), :] = x1_ref[pl.ds(t, 1), :] + mix
        return carry

    lax.fori_loop(0, tp, body, 0)


def _peer_mix(eidx, gate, zn, x1, uv):
    t, d = zn.shape
    tp = PEER_TILE
    nt = t // tp
    return pl.pallas_call(
        _peer_mix_kernel,
        grid=(nt,),
        in_specs=[pl.BlockSpec(memory_space=pl.ANY),
                  pl.BlockSpec(memory_space=pl.ANY),
                  pl.BlockSpec((1, PEER_ROWS, tp), lambda i: (i, 0, 0)),
                  pl.BlockSpec((tp, d), lambda i: (i, 0)),
                  pl.BlockSpec((tp, d), lambda i: (i, 0))],
        out_specs=pl.BlockSpec((tp, d), lambda i: (i, 0)),
        out_shape=jax.ShapeDtypeStruct((t, d), F32),
        scratch_shapes=[pltpu.SMEM((PEER_ROWS * tp,), I32),
                        pltpu.VMEM((PEER_BUFS, PEER_ROWS, d), I32),
                        pltpu.SemaphoreType.DMA((PEER_BUFS,)),
                        pltpu.SemaphoreType.DMA(())],
        compiler_params=_params(("arbitrary",), 32),
        name="peer_mix",
    )(eidx.reshape(nt, PEER_ROWS * tp), uv, gate, zn, x1)


def _pack_pairs(tab):
    bits = lax.bitcast_convert_type(tab.astype(BF16), jnp.uint16).astype(jnp.uint32)
    half = tab.shape[1] // 2
    return lax.bitcast_convert_type(bits[:, :half] | (bits[:, half:] << 16), I32)


def _prepare(w_in, gdn_a_log_fwd, gdn_a_log_bwd, gdn_dt_bias_fwd, gdn_dt_bias_bwd, mla_w_uq, mla_w_ukv,
             mla_qk_norm_q, mla_qk_norm_k, peer_u, peer_v, **others):
    o = 0
    parts = {}
    for name, width in (("qkvz", 4096), ("small", 32), ("cq", MLA_Q_LORA), ("ckv", MLA_KV_LORA),
                        ("kpe", MLA_ROPE), ("mq", MEM_HEADS * MEM_DH), ("gate", 3 * D_MODEL)):
        parts[name] = w_in[:, o:o + width]
        o += width
    wts = dict(others)
    wts["w_main"] = jnp.concatenate(
        [parts["qkvz"], parts["gate"], parts["mq"], parts["cq"], parts["ckv"]], axis=1).astype(BF16)
    wts["w_small"] = jnp.concatenate(
        [parts["small"], parts["kpe"], jnp.zeros((D_MODEL, LANES - 32 - MLA_ROPE), F32)],
        axis=1).astype(BF16)
    zeros16 = jnp.zeros((LANES - 16,), F32)
    wts["a_lane"] = jnp.concatenate(
        [jnp.exp(gdn_a_log_fwd.astype(F32)), jnp.exp(gdn_a_log_bwd.astype(F32)), zeros16]).reshape(1, LANES)
    wts["dtb_lane"] = jnp.concatenate(
        [gdn_dt_bias_fwd.astype(F32), gdn_dt_bias_bwd.astype(F32), zeros16]).reshape(1, LANES)
    pad_q = jnp.zeros((MLA_Q_LORA, MLA_HEADS, LANES - MLA_QK), F32)
    wts["w_uq"] = jnp.concatenate(
        [mla_w_uq.reshape(MLA_Q_LORA, MLA_HEADS, MLA_QK), pad_q], axis=2
    ).reshape(MLA_Q_LORA, MLA_HEADS * LANES).astype(BF16)
    ukv = mla_w_ukv.reshape(MLA_KV_LORA, MLA_HEADS, MLA_NOPE + MLA_V)
    pad_k = jnp.zeros((MLA_KV_LORA, MLA_HEADS, LANES - MLA_NOPE), F32)
    wts["w_uk"] = jnp.concatenate([ukv[:, :, :MLA_NOPE], pad_k], axis=2).reshape(
        MLA_KV_LORA, MLA_HEADS * LANES).astype(BF16)
    wts["w_uv"] = ukv[:, :, MLA_NOPE:].reshape(MLA_KV_LORA, MLA_HEADS * MLA_V).astype(BF16)
    pad_n = jnp.zeros((LANES - MLA_QK,), F32)
    wts["qk_norm_q"] = jnp.concatenate([mla_qk_norm_q, pad_n]).reshape(1, LANES)
    wts["qk_norm_k"] = jnp.concatenate([mla_qk_norm_k, pad_n]).reshape(1, LANES)
    for name in ("mla_q_norm", "mla_kv_norm", "mem_qk_norm_q", "mem_qk_norm_k", "g_ffn"):
        wts[name] = wts[name].reshape(1, -1)
    for name in ("w_branch_gdn", "w_branch_mla", "w_branch_mem", "w_out", "peer_w_q", "mem_w_kv",
                 "peer_keys_a", "peer_keys_b"):
        wts[name] = wts[name].astype(BF16)
    wts["peer_uv"] = jnp.concatenate([_pack_pairs(peer_u), _pack_pairs(peer_v)], axis=1)
    return wts


def _rope_tables(s):
    half = MLA_ROPE // 2
    inv = ROPE_THETA ** (-jnp.arange(half, dtype=F32) / half)
    ang = jnp.arange(s).astype(F32)[:, None] * inv[None, :]
    cos = jnp.cos(ang)
    sin = jnp.sin(ang)
    ones = jnp.ones((s, MLA_NOPE), F32)
    pad = jnp.zeros((s, LANES - MLA_QK), F32)
    zeros_n = jnp.zeros((s, MLA_NOPE), F32)
    zeros_h = jnp.zeros((s, half), F32)
    cos_t = jnp.concatenate([ones, cos, cos, pad], axis=1)
    sin_a = jnp.concatenate([zeros_n, zeros_h, sin, pad], axis=1)
    sin_b = jnp.concatenate([zeros_n, -sin, zeros_h, pad], axis=1)
    return cos_t, sin_a, sin_b


def _layer(x, mem, wts):
    b, s, d = x.shape
    t = b * s
    p_main, p_small = _norm_proj(x.reshape(t, d), wts["g_mix"], wts["w_main"], wts["w_small"], 5, BF16)
    p3 = p_main.reshape(b, s, N_MAIN)
    ps3 = p_small.reshape(b, s, LANES)
    qkv = _gdn_conv(p3, wts["gdn_conv_w"])
    o_fwd = _gdn_scan(qkv, ps3, wts["a_lane"], wts["dtb_lane"], False)
    o_gdn = _gdn_scan(qkv, ps3, wts["a_lane"], wts["dtb_lane"], True, o_fwd, p3, wts["gdn_out_norm"])
    q_m, k_m, v_m = _mla_prep(p3, ps3, _rope_tables(s), wts)
    o_mla = _attention(q_m, k_m, v_m)
    (mkv,) = _norm_proj(mem.reshape(b * MEM_TOKENS, d), wts["mem_norm"], wts["mem_w_kv"], None, 1, F32)
    mkv = mkv.reshape(b, MEM_TOKENS, 2 * MEM_HEADS * MEM_DH)
    x1, zn, qq = _merge(x, o_gdn, o_mla, p3, mkv, wts)
    eidx, gate = _peer_topk(qq.reshape(t, -1), wts["peer_keys_a"], wts["peer_keys_b"])
    y = _peer_mix(eidx, gate, zn.reshape(t, d), x1.reshape(t, d), wts["peer_uv"])
    return y.reshape(b, s, d)


def kernel(x_prompt, x_sample, mem_prompt, mem_sample, g_mix, w_in, gdn_conv_w, gdn_a_log_fwd,
           gdn_a_log_bwd, gdn_dt_bias_fwd, gdn_dt_bias_bwd, gdn_out_norm, mla_q_norm, mla_w_uq,
           mla_kv_norm, mla_w_ukv, mla_qk_norm_q, mla_qk_norm_k, mem_norm, mem_w_kv, mem_qk_norm_q,
           mem_qk_norm_k, w_branch_gdn, w_branch_mla, w_branch_mem, w_out, g_ffn, peer_w_q,
           peer_keys_a, peer_keys_b, peer_u, peer_v):
    wts = _prepare(
        w_in, gdn_a_log_fwd, gdn_a_log_bwd, gdn_dt_bias_fwd, gdn_dt_bias_bwd, mla_w_uq, mla_w_ukv,
        mla_qk_norm_q, mla_qk_norm_k, peer_u, peer_v,
        g_mix=g_mix, gdn_conv_w=gdn_conv_w, gdn_out_norm=gdn_out_norm, mla_q_norm=mla_q_norm,
        mla_kv_norm=mla_kv_norm, mem_norm=mem_norm, mem_w_kv=mem_w_kv, mem_qk_norm_q=mem_qk_norm_q,
        mem_qk_norm_k=mem_qk_norm_k, w_branch_gdn=w_branch_gdn, w_branch_mla=w_branch_mla,
        w_branch_mem=w_branch_mem, w_out=w_out, g_ffn=g_ffn, peer_w_q=peer_w_q,
        peer_keys_a=peer_keys_a, peer_keys_b=peer_keys_b)
    return (_layer(x_prompt, mem_prompt, wts), _layer(x_sample, mem_sample, wts))
```

```python
import functools

import jax
import jax.numpy as jnp
from jax import lax
from jax.experimental import pallas as pl
from jax.experimental.pallas import tpu as pltpu

F32 = jnp.float32
BF16 = jnp.bfloat16
I32 = jnp.int32

D_MODEL = 1024
GDN_HEADS = 8
GDN_DK = 128
GDN_DV = 128
GDN_CONV = 5
GDN_CHUNK = 64
MLA_HEADS = 8
MLA_NOPE = 64
MLA_ROPE = 32
MLA_QK = MLA_NOPE + MLA_ROPE
MLA_V = 64
MLA_Q_LORA = 384
MLA_KV_LORA = 256
ROPE_THETA = 10000.0
MEM_TOKENS = 256
MEM_HEADS = 4
MEM_DH = 128
PEER_HEADS = 8
PEER_DK = 256
PEER_KEYS = 128
PEER_TOPK = 16
NORM_EPS = 1e-6
LOG2E = 1.4426950408889634

LANES = 128
MIB = 1024 * 1024

C_Q, C_K, C_V, C_Z = 0, 1024, 2048, 3072
C_GATE = 4096
C_MQ = 7168
C_LAT = 7680
N_MAIN = 8320
L_AF, L_AB, L_BF, L_BB, L_KPE = 0, 8, 16, 24, 32

PEER_TILE = 128
PEER_ROWS = PEER_HEADS * PEER_TOPK
PEER_BUFS = 8


def _params(sem, vmem_mib):
    return pltpu.CompilerParams(dimension_semantics=sem, vmem_limit_bytes=vmem_mib * MIB)


def _mm(a, b):
    return jnp.dot(a.astype(BF16), b.astype(BF16), preferred_element_type=F32)


def _mm_nt(a, b):
    return lax.dot_general(a.astype(BF16), b.astype(BF16), (((1,), (1,)), ((), ())),
                           preferred_element_type=F32)


def _mm_tn(a, b):
    return lax.dot_general(a.astype(BF16), b.astype(BF16), (((0,), (0,)), ((), ())),
                           preferred_element_type=F32)


def _rms(x, n=None):
    n = x.shape[-1] if n is None else n
    return x * lax.rsqrt(jnp.sum(x * x, axis=-1, keepdims=True) * (1.0 / n) + NORM_EPS)


def _norm_proj_kernel(has_small, x_ref, g_ref, w_ref, *rest):
    if has_small:
        ws_ref, o_ref, os_ref, h_scr = rest
    else:
        o_ref, h_scr = rest

    @pl.when(pl.program_id(1) == 0)
    def _():
        h = _rms(x_ref[...]) * g_ref[...]
        h_scr[...] = h.astype(BF16)
        if has_small:
            os_ref[...] = jnp.dot(h_scr[...], ws_ref[...], preferred_element_type=F32)

    o_ref[...] = jnp.dot(h_scr[...], w_ref[...], preferred_element_type=F32).astype(o_ref.dtype)


def _norm_proj(x2d, g, w, w_small, n_col_blocks, out_dtype):
    t, d = x2d.shape
    n = w.shape[1]
    tm = min(1024, t)
    tn = n // n_col_blocks
    in_specs = [pl.BlockSpec((tm, d), lambda i, j: (i, 0)),
                pl.BlockSpec((1, d), lambda i, j: (0, 0)),
                pl.BlockSpec((d, tn), lambda i, j: (0, j))]
    out_shape = [jax.ShapeDtypeStruct((t, n), out_dtype)]
    out_specs = [pl.BlockSpec((tm, tn), lambda i, j: (i, j))]
    args = [x2d, g.reshape(1, d), w]
    if w_small is not None:
        in_specs.append(pl.BlockSpec((d, LANES), lambda i, j: (0, 0)))
        out_shape.append(jax.ShapeDtypeStruct((t, LANES), F32))
        out_specs.append(pl.BlockSpec((tm, LANES), lambda i, j: (i, 0)))
        args.append(w_small)
    return pl.pallas_call(
        functools.partial(_norm_proj_kernel, w_small is not None),
        grid=(t // tm, n_col_blocks),
        in_specs=in_specs, out_specs=out_specs, out_shape=out_shape,
        scratch_shapes=[pltpu.VMEM((tm, d), BF16)],
        compiler_params=_params(("parallel", "arbitrary"), 48),
        name="norm_proj",
    )(*args)


def _gdn_conv_kernel(ts, x_ref, w_ref, o_ref):
    s = x_ref.shape[1]
    n_chunks = s // ts
    halo = 16
    n_ext = ts + 2 * halo
    is_qk = pl.program_id(1) < 2 * GDN_HEADS
    w = w_ref[...]

    def body(i, carry):
        t0 = pl.multiple_of(i * ts, ts)
        main = x_ref[0, pl.ds(t0, ts), :].astype(F32)
        p0 = pl.multiple_of(jnp.maximum(t0 - halo, 0), halo)
        n0 = pl.multiple_of(jnp.minimum(t0 + ts, s - halo), halo)
        prev = jnp.where(i > 0, x_ref[0, pl.ds(p0, halo), :].astype(F32), 0.0)
        nxt = jnp.where(i < n_chunks - 1, x_ref[0, pl.ds(n0, halo), :].astype(F32), 0.0)
        ext = jnp.concatenate([prev, main, nxt], axis=0)
        acc = w[2:3, :] * ext
        for j in (0, 1, 3, 4):
            acc = acc + w[j:j + 1, :] * pltpu.roll(ext, (2 - j) % n_ext, 0)
        y = acc[halo:halo + ts]
        y = y * jax.nn.sigmoid(y)
        y_n = y * lax.rsqrt(jnp.sum(y * y, axis=-1, keepdims=True) + NORM_EPS)
        o_ref[0, 0, pl.ds(t0, ts), :] = jnp.where(is_qk, y_n, y).astype(o_ref.dtype)
        return carry

    lax.fori_loop(0, n_chunks, body, 0)


def _gdn_conv(p3, conv_w):
    b, s, _ = p3.shape
    ts = min(512, s)
    n_blk = 3 * GDN_HEADS
    return pl.pallas_call(
        functools.partial(_gdn_conv_kernel, ts),
        grid=(b, n_blk),
        in_specs=[pl.BlockSpec((1, s, LANES), lambda bi, c: (bi, 0, c)),
                  pl.BlockSpec((GDN_CONV, LANES), lambda bi, c: (0, c))],
        out_specs=pl.BlockSpec((1, 1, s, LANES), lambda bi, c: (bi, c, 0, 0)),
        out_shape=jax.ShapeDtypeStruct((b, n_blk, s, LANES), BF16),
        compiler_params=_params(("parallel", "parallel"), 40),
        name="gdn_conv",
    )(p3, conv_w)


def _gdn_scan_kernel(reverse, hb, ts, *refs):
    if reverse:
        q_ref, k_ref, v_ref, ps_ref, al_ref, dtb_ref, of_ref, z_ref, gn_ref, o_ref, s_scr = refs
    else:
        q_ref, k_ref, v_ref, ps_ref, al_ref, dtb_ref, o_ref, s_scr = refs
    c = GDN_CHUNK
    n_chunks = ts // c

    @pl.when(pl.program_id(2) == 0)
    def _():
        s_scr[...] = jnp.zeros_like(s_scr)

    ps = ps_ref[0]
    g = -al_ref[...] * jax.nn.softplus(ps + dtb_ref[...])
    beta = jax.nn.sigmoid(ps)
    rowid = lax.broadcasted_iota(I32, (ts, LANES), 0) & (c - 1)
    gc = g
    for kk in (1, 2, 4, 8, 16, 32):
        if reverse:
            gc = gc + jnp.where(rowid < c - kk, pltpu.roll(gc, ts - kk, 0), 0.0)
        else:
            gc = gc + jnp.where(rowid >= kk, pltpu.roll(gc, kk, 0), 0.0)
    shift = (LANES - pl.program_id(1) * hb) % LANES
    gc = pltpu.roll(gc, shift, 1)
    beta = pltpu.roll(beta, shift, 1)
    eg = jnp.exp(gc)
    gc_t = [gc[p * LANES:(p + 1) * LANES, :].T for p in range(ts // LANES)]

    r = lax.broadcasted_iota(I32, (c, c), 0)
    cc = lax.broadcasted_iota(I32, (c, c), 1)
    if reverse:
        r, cc = cc, r
    incl = r >= cc
    strict = r > cc
    eye = jnp.where(r == cc, 1.0, 0.0)
    level_f = [jnp.where((((r >> s) & 1) == 1) & ((cc >> s) == (r >> s) - 1), 1.0, 0.0)
               for s in range(6)]

    lane_g = L_AB if reverse else L_AF
    lane_b = L_BB if reverse else L_BF
    scale = GDN_DK ** -0.5
    order = list(range(n_chunks - 1, -1, -1)) if reverse else list(range(n_chunks))
    units = [(ci, hh) for ci in order for hh in range(hb)]

    kb16, k16, rhs, dec, kd16, a2q, gl = {}, {}, {}, {}, {}, {}, {}
    for un in units:
        ci, hh = un
        r0 = ci * c
        q = q_ref[0, hh, r0:r0 + c, :].astype(F32) * scale
        k = k_ref[0, hh, r0:r0 + c, :].astype(F32)
        v = v_ref[0, hh, r0:r0 + c, :].astype(F32)
        la = lane_g + hh
        lb = lane_b + hh
        gc_c = gc[r0:r0 + c, la:la + 1]
        eg_c = eg[r0:r0 + c, la:la + 1]
        b_c = beta[r0:r0 + c, lb:lb + 1]
        gc_r = gc_t[r0 // LANES][la:la + 1, r0 % LANES:r0 % LANES + c]
        r_last = r0 if reverse else r0 + c - 1
        gc_l = gc[r_last:r_last + 1, la:la + 1]
        dec[un] = jnp.exp(jnp.where(incl, gc_c - gc_r, -1e30))
        kb = k * b_c
        kb16[un] = jnp.concatenate([kb, q], axis=0).astype(BF16)
        k16[un] = k.astype(BF16)
        rhs[un] = jnp.concatenate([v * b_c, kb * eg_c], axis=1).astype(BF16)
        a2q[un] = (q * eg_c).astype(BF16)
        kd16[un] = (k * jnp.exp(gc_l - gc_c)).astype(BF16)
        gl[un] = jnp.exp(gc_l)
    kq = {un: lax.dot_general(kb16[un], k16[un], (((1,), (1,)), ((), ())),
                              preferred_element_type=F32) for un in units}
    lm = {un: jnp.where(strict, kq[un][:c] * dec[un], 0.0) for un in units}
    intra = {un: jnp.where(incl, kq[un][c:] * dec[un], 0.0).astype(BF16) for un in units}
    dm = {un: eye - lm[un] * level_f[0] for un in units}
    for s in range(1, 6):
        cd = {un: _mm(lm[un] * level_f[s], dm[un]) for un in units}
        dm = {un: dm[un] - _mm(dm[un], cd[un]) for un in units}
    sol = {un: _mm(dm[un], rhs[un]) for un in units}

    st = [s_scr[hh] for hh in range(hb)]
    for ci in order:
        r0 = ci * c
        heads = [(ci, hh) for hh in range(hb)]
        a2 = [jnp.concatenate([sol[un][:, GDN_DV:].astype(BF16), a2q[un]], axis=0) for un in heads]
        ws_qs = [jnp.dot(a2[hh], st[hh].astype(BF16), preferred_element_type=F32) for hh in range(hb)]
        v_new = [sol[un][:, :GDN_DV] - ws_qs[un[1]][:c] for un in heads]
        v16 = [vn.astype(BF16) for vn in v_new]
        o = [ws_qs[hh][c:] + jnp.dot(intra[(ci, hh)], v16[hh], preferred_element_type=F32)
             for hh in range(hb)]
        st = [st[hh] * gl[(ci, hh)]
              + lax.dot_general(kd16[(ci, hh)], v16[hh], (((0,), (0,)), ((), ())),
                                preferred_element_type=F32) for hh in range(hb)]
        for hh in range(hb):
            cols = slice(hh * GDN_DV, (hh + 1) * GDN_DV)
            if reverse:
                tot = o[hh] + of_ref[0, r0:r0 + c, cols]
                zz = z_ref[0, r0:r0 + c, cols].astype(F32)
                y = _rms(tot) * gn_ref[...] * (zz * jax.nn.sigmoid(zz))
                o_ref[0, r0:r0 + c, cols] = y.astype(o_ref.dtype)
            else:
                o_ref[0, r0:r0 + c, cols] = o[hh]
    for hh in range(hb):
        s_scr[hh] = st[hh]


def _gdn_scan(qkv, ps3, a_lane, dtb_lane, reverse, o_fwd=None, p3=None, out_norm=None):
    b, _, s, _ = qkv.shape
    hb = 4
    ts = 256
    nt = s // ts
    n_hblk = GDN_HEADS // hb
    tile = (lambda i: nt - 1 - i) if reverse else (lambda i: i)
    in_specs = [
        pl.BlockSpec((1, hb, ts, LANES), lambda bi, h, i: (bi, h, tile(i), 0)),
        pl.BlockSpec((1, hb, ts, LANES), lambda bi, h, i: (bi, n_hblk + h, tile(i), 0)),
        pl.BlockSpec((1, hb, ts, LANES), lambda bi, h, i: (bi, 2 * n_hblk + h, tile(i), 0)),
        pl.BlockSpec((1, ts, LANES), lambda bi, h, i: (bi, tile(i), 0)),
        pl.BlockSpec((1, LANES), lambda bi, h, i: (0, 0)),
        pl.BlockSpec((1, LANES), lambda bi, h, i: (0, 0)),
    ]
    args = [qkv, qkv, qkv, ps3, a_lane, dtb_lane]
    wide = hb * GDN_DV
    if reverse:
        in_specs += [
            pl.BlockSpec((1, ts, wide), lambda bi, h, i: (bi, tile(i), h)),
            pl.BlockSpec((1, ts, wide), lambda bi, h, i: (bi, tile(i), C_Z // wide + h)),
            pl.BlockSpec((1, GDN_DV), lambda bi, h, i: (0, 0)),
        ]
        args += [o_fwd, p3, out_norm.reshape(1, GDN_DV)]
    out_dtype = BF16 if reverse else F32
    return pl.pallas_call(
        functools.partial(_gdn_scan_kernel, reverse, hb, ts),
        grid=(b, n_hblk, nt),
        in_specs=in_specs,
        out_specs=pl.BlockSpec((1, ts, wide), lambda bi, h, i: (bi, tile(i), h)),
        out_shape=jax.ShapeDtypeStruct((b, s, GDN_HEADS * GDN_DV), out_dtype),
        scratch_shapes=[pltpu.VMEM((hb, GDN_DK, GDN_DV), F32)],
        compiler_params=_params(("parallel", "parallel", "arbitrary"), 32),
        name="gdn_scan_bwd" if reverse else "gdn_scan_fwd",
    )(*args)


def _mla_prep_kernel(lat_ref, ps_ref, cos_ref, sa_ref, sb_ref, gq_ref, gkv_ref, wuq_ref, wuk_ref,
                     wuv_ref, nq_ref, nk_ref, q_ref, k_ref, v_ref):
    lat = lat_ref[0].astype(F32)
    cq = _rms(lat[:, :MLA_Q_LORA]) * gq_ref[...]
    ckv = _rms(lat[:, MLA_Q_LORA:]) * gkv_ref[...]
    q_all = _mm(cq, wuq_ref[...])
    k_all = _mm(ckv, wuk_ref[...])
    v_ref[0] = _mm(ckv, wuv_ref[...]).astype(v_ref.dtype)
    ps = ps_ref[0]
    lane = lax.broadcasted_iota(I32, ps.shape, 1)
    in_rope = (lane >= MLA_NOPE) & (lane < MLA_QK)
    kpe = jnp.where(in_rope, pltpu.roll(ps, MLA_NOPE - L_KPE, 1), 0.0)
    cos = cos_ref[...]
    sin_a = sa_ref[...]
    sin_b = sb_ref[...]
    half = MLA_ROPE // 2

    def norm_rope(x, gain):
        y = _rms(x, MLA_QK) * gain
        return y * cos + pltpu.roll(y, half, 1) * sin_a + pltpu.roll(y, LANES - half, 1) * sin_b

    for h in range(MLA_HEADS):
        cols = slice(h * LANES, (h + 1) * LANES)
        q_ref[0, h] = (norm_rope(q_all[:, cols], nq_ref[...]) * (MLA_QK ** -0.5 * LOG2E)).astype(q_ref.dtype)
        k_ref[0, h] = norm_rope(k_all[:, cols] + kpe, nk_ref[...]).astype(k_ref.dtype)


def _mla_prep(p3, ps3, rope_tabs, wts):
    b, s, _ = p3.shape
    tm = min(512, s)
    lat_w = MLA_Q_LORA + MLA_KV_LORA
    full = lambda shape: pl.BlockSpec(shape, lambda bi, i: tuple(0 for _ in shape))
    tab = pl.BlockSpec((tm, LANES), lambda bi, i: (i, 0))
    hw = MLA_HEADS * LANES
    return pl.pallas_call(
        _mla_prep_kernel,
        grid=(b, s // tm),
        in_specs=[pl.BlockSpec((1, tm, lat_w), lambda bi, i: (bi, i, C_LAT // lat_w)),
                  pl.BlockSpec((1, tm, LANES), lambda bi, i: (bi, i, 0)),
                  tab, tab, tab,
                  full((1, MLA_Q_LORA)), full((1, MLA_KV_LORA)),
                  full((MLA_Q_LORA, hw)), full((MLA_KV_LORA, hw)),
                  full((MLA_KV_LORA, MLA_HEADS * MLA_V)),
                  full((1, LANES)), full((1, LANES))],
        out_specs=[pl.BlockSpec((1, MLA_HEADS, tm, LANES), lambda bi, i: (bi, 0, i, 0)),
                   pl.BlockSpec((1, MLA_HEADS, tm, LANES), lambda bi, i: (bi, 0, i, 0)),
                   pl.BlockSpec((1, tm, MLA_HEADS * MLA_V), lambda bi, i: (bi, i, 0))],
        out_shape=[jax.ShapeDtypeStruct((b, MLA_HEADS, s, LANES), BF16),
                   jax.ShapeDtypeStruct((b, MLA_HEADS, s, LANES), BF16),
                   jax.ShapeDtypeStruct((b, s, MLA_HEADS * MLA_V), BF16)],
        compiler_params=_params(("parallel", "parallel"), 40),
        name="mla_prep",
    )(p3, ps3, *rope_tabs, wts["mla_q_norm"], wts["mla_kv_norm"], wts["w_uq"], wts["w_uk"],
      wts["w_uv"], wts["qk_norm_q"], wts["qk_norm_k"])


def _attn_kernel(tk_sub, q_ref, k_ref, v_ref, o_ref, m_scr, l_scr, acc_scr):
    j = pl.program_id(3)

    @pl.when(j == 0)
    def _():
        m_scr[...] = jnp.full_like(m_scr, -jnp.inf)
        l_scr[...] = jnp.zeros_like(l_scr)
        acc_scr[...] = jnp.zeros_like(acc_scr)

    n_sub = k_ref.shape[2] // tk_sub
    n_lt = tk_sub // LANES
    m = [m_scr[hh] for hh in range(2)]
    l = [l_scr[hh] for hh in range(2)]
    acc = [acc_scr[hh] for hh in range(2)]
    for sb in range(n_sub):
        rows = slice(sb * tk_sub, (sb + 1) * tk_sub)
        v = v_ref[0, rows, :]
        for hh in range(2):
            s = lax.dot_general(q_ref[0, hh], k_ref[0, hh, rows, :], (((1,), (1,)), ((), ())),
                                preferred_element_type=F32)
            tiles = [s[:, t * LANES:(t + 1) * LANES] for t in range(n_lt)]
            mx = tiles[0]
            for t in tiles[1:]:
                mx = jnp.maximum(mx, t)
            m_new = jnp.maximum(m[hh], jnp.max(mx, axis=-1, keepdims=True))
            alpha = jnp.exp2(m[hh] - m_new)
            ps = [jnp.exp2(t - m_new) for t in tiles]
            psum = ps[0]
            for t in ps[1:]:
                psum = psum + t
            l[hh] = alpha * l[hh] + psum
            p = jnp.concatenate(ps, axis=1).astype(BF16)
            acc[hh] = alpha * acc[hh] + jnp.dot(p, v, preferred_element_type=F32)
            m[hh] = m_new
    for hh in range(2):
        m_scr[hh] = m[hh]
        l_scr[hh] = l[hh]
        acc_scr[hh] = acc[hh]

    @pl.when(j == pl.num_programs(3) - 1)
    def _():
        lane = lax.broadcasted_iota(I32, o_ref.shape[1:], 1)
        o0 = acc[0] / jnp.sum(l[0], axis=-1, keepdims=True)
        o1 = acc[1] / jnp.sum(l[1], axis=-1, keepdims=True)
        o_ref[0] = jnp.where(lane < MLA_V, o0, o1).astype(o_ref.dtype)


def _attention(q, k, v):
    b, h, s, _ = q.shape
    tq = min(1024, s)
    tk = min(2048, s)
    tk_sub = min(512, tk)
    return pl.pallas_call(
        functools.partial(_attn_kernel, tk_sub),
        grid=(b, h // 2, s // tq, s // tk),
        in_specs=[pl.BlockSpec((1, 2, tq, LANES), lambda bi, p, i, j: (bi, p, i, 0)),
                  pl.BlockSpec((1, 2, tk, LANES), lambda bi, p, i, j: (bi, p, j, 0)),
                  pl.BlockSpec((1, tk, LANES), lambda bi, p, i, j: (bi, j, p))],
        out_specs=pl.BlockSpec((1, tq, LANES), lambda bi, p, i, j: (bi, i, p)),
        out_shape=jax.ShapeDtypeStruct((b, s, h * MLA_V), BF16),
        scratch_shapes=[pltpu.VMEM((2, tq, LANES), F32), pltpu.VMEM((2, tq, LANES), F32),
                        pltpu.VMEM((2, tq, LANES), F32)],
        compiler_params=_params(("parallel", "parallel", "parallel", "arbitrary"), 40),
        name="mla_attention",
    )(q, k, v)


def _merge_kernel(x_ref, og_ref, om_ref, g0_ref, g1_ref, g2_ref, mq_ref, mkv_ref, wbg_ref, wbm_ref,
                  wbe_ref, wo_ref, wq_ref, nq_ref, nk_ref, gf_ref, x1_ref, zn_ref, qq_ref):
    mq = mq_ref[0].astype(F32)
    mkv = mkv_ref[0]
    kv_w = MEM_HEADS * MEM_DH
    outs = []
    for h in range(MEM_HEADS):
        cols = slice(h * MEM_DH, (h + 1) * MEM_DH)
        qh = _rms(mq[:, cols]) * nq_ref[...]
        kh = _rms(mkv[:, cols]) * nk_ref[...]
        vh = mkv[:, kv_w + h * MEM_DH:kv_w + (h + 1) * MEM_DH]
        s = _mm_nt(qh, kh) * (MEM_DH ** -0.5)
        p = jnp.exp(s - jnp.max(s, axis=-1, keepdims=True))
        p = p / jnp.sum(p, axis=-1, keepdims=True)
        outs.append(_mm(p, vh))
    y_mem = _mm(jnp.concatenate(outs, axis=1), wbe_ref[...])
    y_gdn = jnp.dot(og_ref[0], wbg_ref[...], preferred_element_type=F32)
    y_mla = jnp.dot(om_ref[0], wbm_ref[...], preferred_element_type=F32)
    merged = (jax.nn.sigmoid(g0_ref[0].astype(F32)) * y_gdn
              + jax.nn.sigmoid(g1_ref[0].astype(F32)) * y_mla
              + jax.nn.sigmoid(g2_ref[0].astype(F32)) * y_mem)
    x1 = x_ref[0] + _mm(merged, wo_ref[...])
    x1_ref[0] = x1
    zn = _rms(x1) * gf_ref[...]
    zn_ref[0] = zn
    qq_ref[0] = _mm(zn, wq_ref[...])


def _merge(x, o_gdn, o_mla, p3, mkv, wts):
    b, s, d = x.shape
    tm = min(256, s)
    full = lambda shape: pl.BlockSpec(shape, lambda bi, i: tuple(0 for _ in shape))
    tok = lambda w, cb: pl.BlockSpec((1, tm, w), lambda bi, i: (bi, i, cb))
    nq = PEER_HEADS * PEER_DK
    kv_w = MEM_HEADS * MEM_DH
    return pl.pallas_call(
        _merge_kernel,
        grid=(b, s // tm),
        in_specs=[tok(d, 0), tok(d, 0), tok(MLA_HEADS * MLA_V, 0),
                  tok(d, C_GATE // d), tok(d, C_GATE // d + 1), tok(d, C_GATE // d + 2),
                  tok(kv_w, C_MQ // kv_w),
                  pl.BlockSpec((1, MEM_TOKENS, 2 * kv_w), lambda bi, i: (bi, 0, 0)),
                  full((d, d)), full((MLA_HEADS * MLA_V, d)), full((kv_w, d)), full((d, d)),
                  full((d, nq)), full((1, MEM_DH)), full((1, MEM_DH)), full((1, d))],
        out_specs=[tok(d, 0), tok(d, 0), tok(nq, 0)],
        out_shape=[jax.ShapeDtypeStruct((b, s, d), F32), jax.ShapeDtypeStruct((b, s, d), F32),
                   jax.ShapeDtypeStruct((b, s, nq), F32)],
        compiler_params=_params(("parallel", "parallel"), 52),
        name="merge",
    )(x, o_gdn, o_mla, p3, p3, p3, p3, mkv, wts["w_branch_gdn"], wts["w_branch_mla"],
      wts["w_branch_mem"], wts["w_out"], wts["peer_w_q"], wts["mem_qk_norm_q"],
      wts["mem_qk_norm_k"], wts["g_ffn"])


def _top16(cur, order_key=None, payload=None):
    n, t = cur.shape
    if order_key is None:
        order_key = lax.broadcasted_iota(I32, (n, t), 0)
    slot = lax.broadcasted_iota(I32, (PEER_TOPK, t), 0)
    vals = jnp.zeros((PEER_TOPK, t), F32)
    picks = jnp.zeros((PEER_TOPK, t), I32)
    for r in range(PEER_TOPK):
        m = jnp.max(cur, axis=0, keepdims=True)
        am = jnp.min(jnp.where(cur == m, order_key, jnp.int32(2 ** 30)), axis=0, keepdims=True)
        hit = order_key == am
        pick = am if payload is None else jnp.max(jnp.where(hit, payload, -1), axis=0, keepdims=True)
        vals = jnp.where(slot == r, m, vals)
        picks = jnp.where(slot == r, pick, picks)
        cur = jnp.where(hit, -jnp.inf, cur)
    return vals, picks


def _staircase(x1, x2):
    lo1, hi1, lo2, hi2 = x1[0:8], x1[8:16], x2[0:8], x2[8:16]
    return jnp.concatenate(
        [x1[0:1] + lo2, x1[0:1] + hi2, x1[1:2] + lo2, x1[2:3] + lo2, x1[3:4] + lo2, x1[4:5] + lo2,
         hi1 + x2[0:1], lo1 + x2[0:1], lo1 + x2[1:2]], axis=0)


def _staircase_layout(t):
    k = lax.broadcasted_iota(I32, (8, t), 0)
    yes = k >= 0
    pos = jnp.concatenate([k, 8 + k, 16 + k, 32 + k, 48 + k, 64 + k, (8 + k) * 16, k * 16, k * 16 + 1],
                          axis=0)
    valid = jnp.concatenate([yes, yes, yes, k < 5, k < 4, k < 3, yes, k >= 5, k >= 5], axis=0)
    return pos, valid


def _peer_topk_kernel(q_ref, ka_ref, kb_ref, eidx_ref, gate_ref):
    half = PEER_DK // 2
    pos, valid = _staircase_layout(q_ref.shape[0])
    for h in range(PEER_HEADS):
        q1 = q_ref[:, h * PEER_DK:h * PEER_DK + half]
        q2 = q_ref[:, h * PEER_DK + half:(h + 1) * PEER_DK]
        s1 = _mm_nt(ka_ref[h], q1)
        s2 = _mm_nt(kb_ref[h], q2)
        v1, i1 = _top16(s1)
        v2, i2 = _top16(s2)
        cand = jnp.where(valid, _staircase(v1, v2), -jnp.inf)
        cidx = _staircase(i1 * PEER_KEYS, i2)
        top_s, eidx = _top16(cand, pos, cidx)
        p = jnp.exp(top_s - top_s[0:1, :])
        rows = slice(h * PEER_TOPK, (h + 1) * PEER_TOPK)
        gate_ref[0, rows, :] = p / jnp.sum(p, axis=0, keepdims=True)
        eidx_ref[0, rows, :] = eidx


def _peer_topk(qq, keys_a, keys_b):
    t, nq = qq.shape
    tp = PEER_TILE
    nt = t // tp
    kspec = pl.BlockSpec((PEER_HEADS, PEER_KEYS, PEER_DK // 2), lambda i: (0, 0, 0))
    return pl.pallas_call(
        _peer_topk_kernel,
        grid=(nt,),
        in_specs=[pl.BlockSpec((tp, nq), lambda i: (i, 0)), kspec, kspec],
        out_specs=[pl.BlockSpec((1, PEER_ROWS, tp), lambda i: (i, 0, 0)),
                   pl.BlockSpec((1, PEER_ROWS, tp), lambda i: (i, 0, 0))],
        out_shape=[jax.ShapeDtypeStruct((nt, PEER_ROWS, tp), I32),
                   jax.ShapeDtypeStruct((nt, PEER_ROWS, tp), F32)],
        compiler_params=_params(("parallel",), 32),
        name="peer_topk",
    )(qq, keys_a, keys_b)


def _peer_mix_kernel(eidx_hbm, uv_flat, uv_rows, gate_ref, zn_ref, x1_ref, o_ref, idx_smem, *rest):
    bufs = rest[:PEER_BUFS]
    sem, isem = rest[PEER_BUFS:]
    tp = zn_ref.shape[0]
    half = D_MODEL // 2
    n_idx = PEER_ROWS * tp
    step = pl.program_id(0)
    cur = step % 2

    def idx_copy(s, b):
        dst = idx_smem.at[pl.ds(pl.multiple_of(b * n_idx, n_idx), n_idx)]
        return pltpu.make_async_copy(eidx_hbm.at[s], dst, isem.at[b])

    @pl.when(step == 0)
    def _():
        idx_copy(0, 0).start()

    idx_copy(step, cur).wait()

    @pl.when(step + 1 < pl.num_programs(0))
    def _():
        idx_copy(step + 1, 1 - cur).start()

    idx_base = cur * n_idx
    gate_t = gate_ref[0]

    def issue(t, slot):
        off = idx_base + t
        for j in range(PEER_ROWS):
            e = idx_smem[off + j * tp]
            src = uv_flat.at[pl.ds(pl.multiple_of(e * D_MODEL, D_MODEL), D_MODEL)]
            pltpu.make_async_copy(src, bufs[slot].at[j], sem.at[slot]).start()

    def wait(slot):
        pltpu.make_async_copy(uv_rows.at[pl.ds(0, PEER_ROWS)], bufs[slot], sem.at[slot]).wait()

    def unpack(words):
        lo = lax.bitcast_convert_type(words << 16, F32)
        hi = lax.bitcast_convert_type(words & jnp.int32(-65536), F32)
        return lo, hi

    def compute(t, slot):
        x = bufs[slot][...]
        z = zn_ref[pl.ds(t, 1), :]
        u_lo, u_hi = unpack(x[:, :half])
        prod = u_lo * z[:, :half] + u_hi * z[:, half:]
        part = prod[:, 0:LANES]
        for cch in range(1, half // LANES):
            part = part + prod[:, cch * LANES:(cch + 1) * LANES]
        dots = jnp.sum(part, axis=1, keepdims=True)
        act = 0.5 * dots * (1.0 + lax.erf(dots * (2.0 ** -0.5)))
        g_col = pltpu.roll(gate_t, (tp - t) % tp, 1)[:, 0:1]
        wgt = g_col * act
        v_lo, v_hi = unpack(x[:, half:])
        mix = jnp.concatenate([jnp.sum(wgt * v_lo, axis=0, keepdims=True),
                               jnp.sum(wgt * v_hi, axis=0, keepdims=True)], axis=1)
        o_ref[pl.ds(t, 1), :] = x1_ref[pl.ds(t, 1), :] + mix

    ahead = PEER_BUFS - 1
    for t in range(ahead):
        issue(t, t)

    def group(g, last):
        for s in range(PEER_BUFS):
            t = g * PEER_BUFS + s
            wait(s)
            if not last or s == 0:
                issue(t + ahead, (s + ahead) % PEER_BUFS)
            compute(t, s)

    n_groups = tp // PEER_BUFS

    def body(g, carry):
        group(g, False)
        return carry

    lax.fori_loop(0, n_groups - 1, body, 0)
    group(n_groups - 1, True)


def _peer_mix(eidx, gate, zn, x1, uv):
    t, d = zn.shape
    tp = PEER_TILE
    nt = t // tp
    return pl.pallas_call(
        _peer_mix_kernel,
        grid=(nt,),
        in_specs=[pl.BlockSpec(memory_space=pl.ANY),
                  pl.BlockSpec(memory_space=pl.ANY),
                  pl.BlockSpec(memory_space=pl.ANY),
                  pl.BlockSpec((1, PEER_ROWS, tp), lambda i: (i, 0, 0)),
                  pl.BlockSpec((tp, d), lambda i: (i, 0)),
                  pl.BlockSpec((tp, d), lambda i: (i, 0))],
        out_specs=pl.BlockSpec((tp, d), lambda i: (i, 0)),
        out_shape=jax.ShapeDtypeStruct((t, d), F32),
        scratch_shapes=([pltpu.SMEM((2 * PEER_ROWS * tp,), I32)]
                        + [pltpu.VMEM((PEER_ROWS, d), I32) for _ in range(PEER_BUFS)]
                        + [pltpu.SemaphoreType.DMA((PEER_BUFS,)), pltpu.SemaphoreType.DMA((2,))]),
        compiler_params=_params(("arbitrary",), 32),
        name="peer_mix",
    )(eidx.reshape(nt, PEER_ROWS * tp), uv.reshape(-1), uv, gate, zn, x1)


def _pack_pairs(tab):
    bits = lax.bitcast_convert_type(tab.astype(BF16), jnp.uint16).astype(jnp.uint32)
    half = tab.shape[1] // 2
    return lax.bitcast_convert_type(bits[:, :half] | (bits[:, half:] << 16), I32)


def _prepare(w_in, gdn_a_log_fwd, gdn_a_log_bwd, gdn_dt_bias_fwd, gdn_dt_bias_bwd, mla_w_uq, mla_w_ukv,
             mla_qk_norm_q, mla_qk_norm_k, peer_u, peer_v, **others):
    o = 0
    parts = {}
    for name, width in (("qkvz", 4096), ("small", 32), ("cq", MLA_Q_LORA), ("ckv", MLA_KV_LORA),
                        ("kpe", MLA_ROPE), ("mq", MEM_HEADS * MEM_DH), ("gate", 3 * D_MODEL)):
        parts[name] = w_in[:, o:o + width]
        o += width
    wts = dict(others)
    wts["w_main"] = jnp.concatenate(
        [parts["qkvz"], parts["gate"], parts["mq"], parts["cq"], parts["ckv"]], axis=1).astype(BF16)
    wts["w_small"] = jnp.concatenate(
        [parts["small"], parts["kpe"], jnp.zeros((D_MODEL, LANES - 32 - MLA_ROPE), F32)],
        axis=1).astype(BF16)
    zeros16 = jnp.zeros((LANES - 16,), F32)
    wts["a_lane"] = jnp.concatenate(
        [jnp.exp(gdn_a_log_fwd.astype(F32)), jnp.exp(gdn_a_log_bwd.astype(F32)), zeros16]).reshape(1, LANES)
    wts["dtb_lane"] = jnp.concatenate(
        [gdn_dt_bias_fwd.astype(F32), gdn_dt_bias_bwd.astype(F32), zeros16]).reshape(1, LANES)
    pad_q = jnp.zeros((MLA_Q_LORA, MLA_HEADS, LANES - MLA_QK), F32)
    wts["w_uq"] = jnp.concatenate(
        [mla_w_uq.reshape(MLA_Q_LORA, MLA_HEADS, MLA_QK), pad_q], axis=2
    ).reshape(MLA_Q_LORA, MLA_HEADS * LANES).astype(BF16)
    ukv = mla_w_ukv.reshape(MLA_KV_LORA, MLA_HEADS, MLA_NOPE + MLA_V)
    pad_k = jnp.zeros((MLA_KV_LORA, MLA_HEADS, LANES - MLA_NOPE), F32)
    wts["w_uk"] = jnp.concatenate([ukv[:, :, :MLA_NOPE], pad_k], axis=2).reshape(
        MLA_KV_LORA, MLA_HEADS * LANES).astype(BF16)
    wts["w_uv"] = ukv[:, :, MLA_NOPE:].reshape(MLA_KV_LORA, MLA_HEADS * MLA_V).astype(BF16)
    pad_n = jnp.zeros((LANES - MLA_QK,), F32)
    wts["qk_norm_q"] = jnp.concatenate([mla_qk_norm_q, pad_n]).reshape(1, LANES)
    wts["qk_norm_k"] = jnp.concatenate([mla_qk_norm_k, pad_n]).reshape(1, LANES)
    for name in ("mla_q_norm", "mla_kv_norm", "mem_qk_norm_q", "mem_qk_norm_k", "g_ffn"):
        wts[name] = wts[name].reshape(1, -1)
    for name in ("w_branch_gdn", "w_branch_mla", "w_branch_mem", "w_out", "peer_w_q", "mem_w_kv",
                 "peer_keys_a", "peer_keys_b"):
        wts[name] = wts[name].astype(BF16)
    wts["peer_uv"] = jnp.concatenate([_pack_pairs(peer_u), _pack_pairs(peer_v)], axis=1)
    return wts


def _rope_tables(s):
    half = MLA_ROPE // 2
    inv = ROPE_THETA ** (-jnp.arange(half, dtype=F32) / half)
    ang = jnp.arange(s).astype(F32)[:, None] * inv[None, :]
    cos = jnp.cos(ang)
    sin = jnp.sin(ang)
    ones = jnp.ones((s, MLA_NOPE), F32)
    pad = jnp.zeros((s, LANES - MLA_QK), F32)
    zeros_n = jnp.zeros((s, MLA_NOPE), F32)
    zeros_h = jnp.zeros((s, half), F32)
    cos_t = jnp.concatenate([ones, cos, cos, pad], axis=1)
    sin_a = jnp.concatenate([zeros_n, zeros_h, sin, pad], axis=1)
    sin_b = jnp.concatenate([zeros_n, -sin, zeros_h, pad], axis=1)
    return cos_t, sin_a, sin_b


def _layer(x, mem, wts):
    b, s, d = x.shape
    t = b * s
    p_main, p_small = _norm_proj(x.reshape(t, d), wts["g_mix"], wts["w_main"], wts["w_small"], 5, BF16)
    p3 = p_main.reshape(b, s, N_MAIN)
    ps3 = p_small.reshape(b, s, LANES)
    qkv = _gdn_conv(p3, wts["gdn_conv_w"])
    o_fwd = _gdn_scan(qkv, ps3, wts["a_lane"], wts["dtb_lane"], False)
    o_gdn = _gdn_scan(qkv, ps3, wts["a_lane"], wts["dtb_lane"], True, o_fwd, p3, wts["gdn_out_norm"])
    q_m, k_m, v_m = _mla_prep(p3, ps3, _rope_tables(s), wts)
    o_mla = _attention(q_m, k_m, v_m)
    (mkv,) = _norm_proj(mem.reshape(b * MEM_TOKENS, d), wts["mem_norm"], wts["mem_w_kv"], None, 1, F32)
    mkv = mkv.reshape(b, MEM_TOKENS, 2 * MEM_HEADS * MEM_DH)
    x1, zn, qq = _merge(x, o_gdn, o_mla, p3, mkv, wts)
    eidx, gate = _peer_topk(qq.reshape(t, -1), wts["peer_keys_a"], wts["peer_keys_b"])
    y = _peer_mix(eidx, gate, zn.reshape(t, d), x1.reshape(t, d), wts["peer_uv"])
    return y.reshape(b, s, d)


def kernel(x_prompt, x_sample, mem_prompt, mem_sample, g_mix, w_in, gdn_conv_w, gdn_a_log_fwd,
           gdn_a_log_bwd, gdn_dt_bias_fwd, gdn_dt_bias_bwd, gdn_out_norm, mla_q_norm, mla_w_uq,
           mla_kv_norm, mla_w_ukv, mla_qk_norm_q, mla_qk_norm_k, mem_norm, mem_w_kv, mem_qk_norm_q,
           mem_qk_norm_k, w_branch_gdn, w_branch_mla, w_branch_mem, w_out, g_ffn, peer_w_q,
           peer_keys_a, peer_keys_b, peer_u, peer_v):
    wts = _prepare(
        w_in, gdn_a_log_fwd, gdn_a_log_bwd, gdn_dt_bias_fwd, gdn_dt_bias_bwd, mla_w_uq, mla_w_ukv,
        mla_qk_norm_q, mla_qk_norm_k, peer_u, peer_v,
        g_mix=g_mix, gdn_conv_w=gdn_conv_w, gdn_out_norm=gdn_out_norm, mla_q_norm=mla_q_norm,
        mla_kv_norm=mla_kv_norm, mem_norm=mem_norm, mem_w_kv=mem_w_kv, mem_qk_norm_q=mem_qk_norm_q,
        mem_qk_norm_k=mem_qk_norm_k, w_branch_gdn=w_branch_gdn, w_branch_mla=w_branch_mla,
        w_branch_mem=w_branch_mem, w_out=w_out, g_ffn=g_ffn, peer_w_q=peer_w_q,
        peer_keys_a=peer_keys_a, peer_keys_b=peer_keys_b)
    return (_layer(x_prompt, mem_prompt, wts), _layer(x_sample, mem_sample, wts))
```

```python
import functools

import jax
import jax.numpy as jnp
from jax import lax
from jax.experimental import pallas as pl
from jax.experimental.pallas import tpu as pltpu

F32 = jnp.float32
BF16 = jnp.bfloat16
I32 = jnp.int32

D_MODEL = 1024
GDN_HEADS = 8
GDN_DK = 128
GDN_DV = 128
GDN_CONV = 5
GDN_CHUNK = 64
MLA_HEADS = 8
MLA_NOPE = 64
MLA_ROPE = 32
MLA_QK = MLA_NOPE + MLA_ROPE
MLA_V = 64
MLA_Q_LORA = 384
MLA_KV_LORA = 256
ROPE_THETA = 10000.0
MEM_TOKENS = 256
MEM_HEADS = 4
MEM_DH = 128
PEER_HEADS = 8
PEER_DK = 256
PEER_KEYS = 128
PEER_TOPK = 16
NORM_EPS = 1e-6
LOG2E = 1.4426950408889634

LANES = 128
MIB = 1024 * 1024

C_Q, C_K, C_V, C_Z = 0, 1024, 2048, 3072
C_GATE = 4096
C_MQ = 7168
C_LAT = 7680
N_MAIN = 8320
L_AF, L_AB, L_BF, L_BB, L_KPE = 0, 8, 16, 24, 32

PEER_TILE = 128
PEER_ROWS = PEER_HEADS * PEER_TOPK
PEER_BUFS = 8


def _params(sem, vmem_mib):
    return pltpu.CompilerParams(dimension_semantics=sem, vmem_limit_bytes=vmem_mib * MIB)


def _mm(a, b):
    return jnp.dot(a.astype(BF16), b.astype(BF16), preferred_element_type=F32)


def _mm_nt(a, b):
    return lax.dot_general(a.astype(BF16), b.astype(BF16), (((1,), (1,)), ((), ())),
                           preferred_element_type=F32)


def _mm_tn(a, b):
    return lax.dot_general(a.astype(BF16), b.astype(BF16), (((0,), (0,)), ((), ())),
                           preferred_element_type=F32)


def _rms(x, n=None):
    n = x.shape[-1] if n is None else n
    return x * lax.rsqrt(jnp.sum(x * x, axis=-1, keepdims=True) * (1.0 / n) + NORM_EPS)


def _norm_proj_kernel(has_small, x_ref, g_ref, w_ref, *rest):
    if has_small:
        ws_ref, o_ref, os_ref, h_scr = rest
    else:
        o_ref, h_scr = rest

    @pl.when(pl.program_id(1) == 0)
    def _():
        h = _rms(x_ref[...]) * g_ref[...]
        h_scr[...] = h.astype(BF16)
        if has_small:
            os_ref[...] = jnp.dot(h_scr[...], ws_ref[...], preferred_element_type=F32)

    o_ref[...] = jnp.dot(h_scr[...], w_ref[...], preferred_element_type=F32).astype(o_ref.dtype)


def _norm_proj(x2d, g, w, w_small, n_col_blocks, out_dtype):
    t, d = x2d.shape
    n = w.shape[1]
    tm = min(1024, t)
    tn = n // n_col_blocks
    in_specs = [pl.BlockSpec((tm, d), lambda i, j: (i, 0)),
                pl.BlockSpec((1, d), lambda i, j: (0, 0)),
                pl.BlockSpec((d, tn), lambda i, j: (0, j))]
    out_shape = [jax.ShapeDtypeStruct((t, n), out_dtype)]
    out_specs = [pl.BlockSpec((tm, tn), lambda i, j: (i, j))]
    args = [x2d, g.reshape(1, d), w]
    if w_small is not None:
        in_specs.append(pl.BlockSpec((d, LANES), lambda i, j: (0, 0)))
        out_shape.append(jax.ShapeDtypeStruct((t, LANES), F32))
        out_specs.append(pl.BlockSpec((tm, LANES), lambda i, j: (i, 0)))
        args.append(w_small)
    return pl.pallas_call(
        functools.partial(_norm_proj_kernel, w_small is not None),
        grid=(t // tm, n_col_blocks),
        in_specs=in_specs, out_specs=out_specs, out_shape=out_shape,
        scratch_shapes=[pltpu.VMEM((tm, d), BF16)],
        compiler_params=_params(("parallel", "arbitrary"), 48),
        name="norm_proj",
    )(*args)


def _gdn_conv_kernel(ts, x_ref, w_ref, o_ref):
    s = x_ref.shape[1]
    n_chunks = s // ts
    halo = 16
    n_ext = ts + 2 * halo
    is_qk = pl.program_id(1) < 2 * GDN_HEADS
    w = w_ref[...]

    def body(i, carry):
        t0 = pl.multiple_of(i * ts, ts)
        main = x_ref[0, pl.ds(t0, ts), :].astype(F32)
        p0 = pl.multiple_of(jnp.maximum(t0 - halo, 0), halo)
        n0 = pl.multiple_of(jnp.minimum(t0 + ts, s - halo), halo)
        prev = jnp.where(i > 0, x_ref[0, pl.ds(p0, halo), :].astype(F32), 0.0)
        nxt = jnp.where(i < n_chunks - 1, x_ref[0, pl.ds(n0, halo), :].astype(F32), 0.0)
        ext = jnp.concatenate([prev, main, nxt], axis=0)
        acc = w[2:3, :] * ext
        for j in (0, 1, 3, 4):
            acc = acc + w[j:j + 1, :] * pltpu.roll(ext, (2 - j) % n_ext, 0)
        y = acc[halo:halo + ts]
        y = y * jax.nn.sigmoid(y)
        y_n = y * lax.rsqrt(jnp.sum(y * y, axis=-1, keepdims=True) + NORM_EPS)
        o_ref[0, 0, pl.ds(t0, ts), :] = jnp.where(is_qk, y_n, y).astype(o_ref.dtype)
        return carry

    lax.fori_loop(0, n_chunks, body, 0)


def _gdn_conv(p3, conv_w):
    b, s, _ = p3.shape
    ts = min(512, s)
    n_blk = 3 * GDN_HEADS
    return pl.pallas_call(
        functools.partial(_gdn_conv_kernel, ts),
        grid=(b, n_blk),
        in_specs=[pl.BlockSpec((1, s, LANES), lambda bi, c: (bi, 0, c)),
                  pl.BlockSpec((GDN_CONV, LANES), lambda bi, c: (0, c))],
        out_specs=pl.BlockSpec((1, 1, s, LANES), lambda bi, c: (bi, c, 0, 0)),
        out_shape=jax.ShapeDtypeStruct((b, n_blk, s, LANES), BF16),
        compiler_params=_params(("parallel", "parallel"), 40),
        name="gdn_conv",
    )(p3, conv_w)


def _gdn_scan_kernel(reverse, hb, ts, *refs):
    if reverse:
        q_ref, k_ref, v_ref, ps_ref, al_ref, dtb_ref, of_ref, z_ref, gn_ref, o_ref, s_scr = refs
    else:
        q_ref, k_ref, v_ref, ps_ref, al_ref, dtb_ref, o_ref, s_scr = refs
    c = GDN_CHUNK
    n_chunks = ts // c

    @pl.when(pl.program_id(2) == 0)
    def _():
        s_scr[...] = jnp.zeros_like(s_scr)

    ps = ps_ref[0]
    g = -al_ref[...] * jax.nn.softplus(ps + dtb_ref[...])
    beta = jax.nn.sigmoid(ps)
    rowid = lax.broadcasted_iota(I32, (ts, LANES), 0) & (c - 1)
    gc = g
    for kk in (1, 2, 4, 8, 16, 32):
        if reverse:
            gc = gc + jnp.where(rowid < c - kk, pltpu.roll(gc, ts - kk, 0), 0.0)
        else:
            gc = gc + jnp.where(rowid >= kk, pltpu.roll(gc, kk, 0), 0.0)
    shift = (LANES - pl.program_id(1) * hb) % LANES
    gc = pltpu.roll(gc, shift, 1)
    beta = pltpu.roll(beta, shift, 1)
    eg = jnp.exp(gc)
    gc_t = [gc[p * LANES:(p + 1) * LANES, :].T for p in range(ts // LANES)]

    r = lax.broadcasted_iota(I32, (c, c), 0)
    cc = lax.broadcasted_iota(I32, (c, c), 1)
    if reverse:
        r, cc = cc, r
    incl = r >= cc
    strict = r > cc
    eye = jnp.where(r == cc, 1.0, 0.0)
    level_f = [jnp.where((((r >> s) & 1) == 1) & ((cc >> s) == (r >> s) - 1), 1.0, 0.0)
               for s in range(6)]

    lane_g = L_AB if reverse else L_AF
    lane_b = L_BB if reverse else L_BF
    scale = GDN_DK ** -0.5
    order = list(range(n_chunks - 1, -1, -1)) if reverse else list(range(n_chunks))
    units = [(ci, hh) for ci in order for hh in range(hb)]

    kb16, k16, rhs, dec, kd16, a2q, gl = {}, {}, {}, {}, {}, {}, {}
    for un in units:
        ci, hh = un
        r0 = ci * c
        q = q_ref[0, hh, r0:r0 + c, :].astype(F32) * scale
        k = k_ref[0, hh, r0:r0 + c, :].astype(F32)
        v = v_ref[0, hh, r0:r0 + c, :].astype(F32)
        la = lane_g + hh
        lb = lane_b + hh
        gc_c = gc[r0:r0 + c, la:la + 1]
        eg_c = eg[r0:r0 + c, la:la + 1]
        b_c = beta[r0:r0 + c, lb:lb + 1]
        gc_r = gc_t[r0 // LANES][la:la + 1, r0 % LANES:r0 % LANES + c]
        r_last = r0 if reverse else r0 + c - 1
        gc_l = gc[r_last:r_last + 1, la:la + 1]
        dec[un] = jnp.exp(jnp.where(incl, gc_c - gc_r, -1e30))
        kb = k * b_c
        kb16[un] = jnp.concatenate([kb, q], axis=0).astype(BF16)
        k16[un] = k.astype(BF16)
        rhs[un] = jnp.concatenate([v * b_c, kb * eg_c], axis=1).astype(BF16)
        a2q[un] = (q * eg_c).astype(BF16)
        kd16[un] = (k * jnp.exp(gc_l - gc_c)).astype(BF16)
        gl[un] = jnp.exp(gc_l)
    kq = {un: lax.dot_general(kb16[un], k16[un], (((1,), (1,)), ((), ())),
                              preferred_element_type=F32) for un in units}
    lm = {un: jnp.where(strict, kq[un][:c] * dec[un], 0.0) for un in units}
    intra = {un: jnp.where(incl, kq[un][c:] * dec[un], 0.0).astype(BF16) for un in units}
    dm = {un: eye - lm[un] * level_f[0] for un in units}
    for s in range(1, 6):
        cd = {un: _mm(lm[un] * level_f[s], dm[un]) for un in units}
        dm = {un: dm[un] - _mm(dm[un], cd[un]) for un in units}
    sol = {un: _mm(dm[un], rhs[un]) for un in units}

    st = [s_scr[hh] for hh in range(hb)]
    for ci in order:
        r0 = ci * c
        heads = [(ci, hh) for hh in range(hb)]
        a2 = [jnp.concatenate([sol[un][:, GDN_DV:].astype(BF16), a2q[un]], axis=0) for un in heads]
        ws_qs = [jnp.dot(a2[hh], st[hh].astype(BF16), preferred_element_type=F32) for hh in range(hb)]
        v_new = [sol[un][:, :GDN_DV] - ws_qs[un[1]][:c] for un in heads]
        v16 = [vn.astype(BF16) for vn in v_new]
        o = [ws_qs[hh][c:] + jnp.dot(intra[(ci, hh)], v16[hh], preferred_element_type=F32)
             for hh in range(hb)]
        st = [st[hh] * gl[(ci, hh)]
              + lax.dot_general(kd16[(ci, hh)], v16[hh], (((0,), (0,)), ((), ())),
                                preferred_element_type=F32) for hh in range(hb)]
        for hh in range(hb):
            cols = slice(hh * GDN_DV, (hh + 1) * GDN_DV)
            if reverse:
                tot = o[hh] + of_ref[0, r0:r0 + c, cols]
                zz = z_ref[0, r0:r0 + c, cols].astype(F32)
                y = _rms(tot) * gn_ref[...] * (zz * jax.nn.sigmoid(zz))
                o_ref[0, r0:r0 + c, cols] = y.astype(o_ref.dtype)
            else:
                o_ref[0, r0:r0 + c, cols] = o[hh]
    for hh in range(hb):
        s_scr[hh] = st[hh]


def _gdn_scan(qkv, ps3, a_lane, dtb_lane, reverse, o_fwd=None, p3=None, out_norm=None):
    b, _, s, _ = qkv.shape
    hb = 4
    ts = 256
    nt = s // ts
    n_hblk = GDN_HEADS // hb
    tile = (lambda i: nt - 1 - i) if reverse else (lambda i: i)
    in_specs = [
        pl.BlockSpec((1, hb, ts, LANES), lambda bi, h, i: (bi, h, tile(i), 0)),
        pl.BlockSpec((1, hb, ts, LANES), lambda bi, h, i: (bi, n_hblk + h, tile(i), 0)),
        pl.BlockSpec((1, hb, ts, LANES), lambda bi, h, i: (bi, 2 * n_hblk + h, tile(i), 0)),
        pl.BlockSpec((1, ts, LANES), lambda bi, h, i: (bi, tile(i), 0)),
        pl.BlockSpec((1, LANES), lambda bi, h, i: (0, 0)),
        pl.BlockSpec((1, LANES), lambda bi, h, i: (0, 0)),
    ]
    args = [qkv, qkv, qkv, ps3, a_lane, dtb_lane]
    wide = hb * GDN_DV
    if reverse:
        in_specs += [
            pl.BlockSpec((1, ts, wide), lambda bi, h, i: (bi, tile(i), h)),
            pl.BlockSpec((1, ts, wide), lambda bi, h, i: (bi, tile(i), C_Z // wide + h)),
            pl.BlockSpec((1, GDN_DV), lambda bi, h, i: (0, 0)),
        ]
        args += [o_fwd, p3, out_norm.reshape(1, GDN_DV)]
    out_dtype = BF16 if reverse else F32
    return pl.pallas_call(
        functools.partial(_gdn_scan_kernel, reverse, hb, ts),
        grid=(b, n_hblk, nt),
        in_specs=in_specs,
        out_specs=pl.BlockSpec((1, ts, wide), lambda bi, h, i: (bi, tile(i), h)),
        out_shape=jax.ShapeDtypeStruct((b, s, GDN_HEADS * GDN_DV), out_dtype),
        scratch_shapes=[pltpu.VMEM((hb, GDN_DK, GDN_DV), F32)],
        compiler_params=_params(("parallel", "parallel", "arbitrary"), 32),
        name="gdn_scan_bwd" if reverse else "gdn_scan_fwd",
    )(*args)


def _mla_prep_kernel(lat_ref, ps_ref, cos_ref, sa_ref, sb_ref, gq_ref, gkv_ref, wuq_ref, wuk_ref,
                     wuv_ref, nq_ref, nk_ref, q_ref, k_ref, v_ref):
    lat = lat_ref[0].astype(F32)
    cq = _rms(lat[:, :MLA_Q_LORA]) * gq_ref[...]
    ckv = _rms(lat[:, MLA_Q_LORA:]) * gkv_ref[...]
    q_all = _mm(cq, wuq_ref[...])
    k_all = _mm(ckv, wuk_ref[...])
    v_ref[0] = _mm(ckv, wuv_ref[...]).astype(v_ref.dtype)
    ps = ps_ref[0]
    lane = lax.broadcasted_iota(I32, ps.shape, 1)
    in_rope = (lane >= MLA_NOPE) & (lane < MLA_QK)
    kpe = jnp.where(in_rope, pltpu.roll(ps, MLA_NOPE - L_KPE, 1), 0.0)
    cos = cos_ref[...]
    sin_a = sa_ref[...]
    sin_b = sb_ref[...]
    half = MLA_ROPE // 2

    def norm_rope(x, gain):
        y = _rms(x, MLA_QK) * gain
        return y * cos + pltpu.roll(y, half, 1) * sin_a + pltpu.roll(y, LANES - half, 1) * sin_b

    for h in range(MLA_HEADS):
        cols = slice(h * LANES, (h + 1) * LANES)
        q_ref[0, h] = (norm_rope(q_all[:, cols], nq_ref[...]) * (MLA_QK ** -0.5 * LOG2E)).astype(q_ref.dtype)
        k_ref[0, h] = norm_rope(k_all[:, cols] + kpe, nk_ref[...]).astype(k_ref.dtype)


def _mla_prep(p3, ps3, rope_tabs, wts):
    b, s, _ = p3.shape
    tm = min(512, s)
    lat_w = MLA_Q_LORA + MLA_KV_LORA
    full = lambda shape: pl.BlockSpec(shape, lambda bi, i: tuple(0 for _ in shape))
    tab = pl.BlockSpec((tm, LANES), lambda bi, i: (i, 0))
    hw = MLA_HEADS * LANES
    return pl.pallas_call(
        _mla_prep_kernel,
        grid=(b, s // tm),
        in_specs=[pl.BlockSpec((1, tm, lat_w), lambda bi, i: (bi, i, C_LAT // lat_w)),
                  pl.BlockSpec((1, tm, LANES), lambda bi, i: (bi, i, 0)),
                  tab, tab, tab,
                  full((1, MLA_Q_LORA)), full((1, MLA_KV_LORA)),
                  full((MLA_Q_LORA, hw)), full((MLA_KV_LORA, hw)),
                  full((MLA_KV_LORA, MLA_HEADS * MLA_V)),
                  full((1, LANES)), full((1, LANES))],
        out_specs=[pl.BlockSpec((1, MLA_HEADS, tm, LANES), lambda bi, i: (bi, 0, i, 0)),
                   pl.BlockSpec((1, MLA_HEADS, tm, LANES), lambda bi, i: (bi, 0, i, 0)),
                   pl.BlockSpec((1, tm, MLA_HEADS * MLA_V), lambda bi, i: (bi, i, 0))],
        out_shape=[jax.ShapeDtypeStruct((b, MLA_HEADS, s, LANES), BF16),
                   jax.ShapeDtypeStruct((b, MLA_HEADS, s, LANES), BF16),
                   jax.ShapeDtypeStruct((b, s, MLA_HEADS * MLA_V), BF16)],
        compiler_params=_params(("parallel", "parallel"), 40),
        name="mla_prep",
    )(p3, ps3, *rope_tabs, wts["mla_q_norm"], wts["mla_kv_norm"], wts["w_uq"], wts["w_uk"],
      wts["w_uv"], wts["qk_norm_q"], wts["qk_norm_k"])


def _attn_kernel(tk_sub, q_ref, k_ref, v_ref, o_ref, m_scr, l_scr, acc_scr):
    j = pl.program_id(3)

    @pl.when(j == 0)
    def _():
        m_scr[...] = jnp.full_like(m_scr, -jnp.inf)
        l_scr[...] = jnp.zeros_like(l_scr)
        acc_scr[...] = jnp.zeros_like(acc_scr)

    n_sub = k_ref.shape[2] // tk_sub
    n_lt = tk_sub // LANES
    m = [m_scr[hh] for hh in range(2)]
    l = [l_scr[hh] for hh in range(2)]
    acc = [acc_scr[hh] for hh in range(2)]
    for sb in range(n_sub):
        rows = slice(sb * tk_sub, (sb + 1) * tk_sub)
        v = v_ref[0, rows, :]
        for hh in range(2):
            s = lax.dot_general(q_ref[0, hh], k_ref[0, hh, rows, :], (((1,), (1,)), ((), ())),
                                preferred_element_type=F32)
            tiles = [s[:, t * LANES:(t + 1) * LANES] for t in range(n_lt)]
            mx = tiles[0]
            for t in tiles[1:]:
                mx = jnp.maximum(mx, t)
            m_new = jnp.maximum(m[hh], jnp.max(mx, axis=-1, keepdims=True))
            alpha = jnp.exp2(m[hh] - m_new)
            ps = [jnp.exp2(t - m_new) for t in tiles]
            psum = ps[0]
            for t in ps[1:]:
                psum = psum + t
            l[hh] = alpha * l[hh] + psum
            p = jnp.concatenate(ps, axis=1).astype(BF16)
            acc[hh] = alpha * acc[hh] + jnp.dot(p, v, preferred_element_type=F32)
            m[hh] = m_new
    for hh in range(2):
        m_scr[hh] = m[hh]
        l_scr[hh] = l[hh]
        acc_scr[hh] = acc[hh]

    @pl.when(j == pl.num_programs(3) - 1)
    def _():
        lane = lax.broadcasted_iota(I32, o_ref.shape[1:], 1)
        o0 = acc[0] / jnp.sum(l[0], axis=-1, keepdims=True)
        o1 = acc[1] / jnp.sum(l[1], axis=-1, keepdims=True)
        o_ref[0] = jnp.where(lane < MLA_V, o0, o1).astype(o_ref.dtype)


def _attention(q, k, v):
    b, h, s, _ = q.shape
    tq = min(512, s)
    tk = min(4096, s)
    tk_sub = min(256, tk)
    return pl.pallas_call(
        functools.partial(_attn_kernel, tk_sub),
        grid=(b, h // 2, s // tq, s // tk),
        in_specs=[pl.BlockSpec((1, 2, tq, LANES), lambda bi, p, i, j: (bi, p, i, 0)),
                  pl.BlockSpec((1, 2, tk, LANES), lambda bi, p, i, j: (bi, p, j, 0)),
                  pl.BlockSpec((1, tk, LANES), lambda bi, p, i, j: (bi, j, p))],
        out_specs=pl.BlockSpec((1, tq, LANES), lambda bi, p, i, j: (bi, i, p)),
        out_shape=jax.ShapeDtypeStruct((b, s, h * MLA_V), BF16),
        scratch_shapes=[pltpu.VMEM((2, tq, LANES), F32), pltpu.VMEM((2, tq, LANES), F32),
                        pltpu.VMEM((2, tq, LANES), F32)],
        compiler_params=_params(("parallel", "parallel", "parallel", "arbitrary"), 40),
        name="mla_attention",
    )(q, k, v)


def _merge_kernel(x_ref, og_ref, om_ref, g0_ref, g1_ref, g2_ref, mq_ref, mkv_ref, wbg_ref, wbm_ref,
                  wbe_ref, wo_ref, wq_ref, nq_ref, nk_ref, gf_ref, x1_ref, zn_ref, qq_ref):
    mq = mq_ref[0].astype(F32)
    mkv = mkv_ref[0]
    kv_w = MEM_HEADS * MEM_DH
    outs = []
    for h in range(MEM_HEADS):
        cols = slice(h * MEM_DH, (h + 1) * MEM_DH)
        qh = _rms(mq[:, cols]) * nq_ref[...]
        kh = _rms(mkv[:, cols]) * nk_ref[...]
        vh = mkv[:, kv_w + h * MEM_DH:kv_w + (h + 1) * MEM_DH]
        s = _mm_nt(qh, kh) * (MEM_DH ** -0.5)
        p = jnp.exp(s - jnp.max(s, axis=-1, keepdims=True))
        p = p / jnp.sum(p, axis=-1, keepdims=True)
        outs.append(_mm(p, vh))
    y_mem = _mm(jnp.concatenate(outs, axis=1), wbe_ref[...])
    y_gdn = jnp.dot(og_ref[0], wbg_ref[...], preferred_element_type=F32)
    y_mla = jnp.dot(om_ref[0], wbm_ref[...], preferred_element_type=F32)
    merged = (jax.nn.sigmoid(g0_ref[0].astype(F32)) * y_gdn
              + jax.nn.sigmoid(g1_ref[0].astype(F32)) * y_mla
              + jax.nn.sigmoid(g2_ref[0].astype(F32)) * y_mem)
    x1 = x_ref[0] + _mm(merged, wo_ref[...])
    x1_ref[0] = x1
    zn = _rms(x1) * gf_ref[...]
    zn_ref[0] = zn
    qq_ref[0] = _mm(zn, wq_ref[...])


def _merge(x, o_gdn, o_mla, p3, mkv, wts):
    b, s, d = x.shape
    tm = min(256, s)
    full = lambda shape: pl.BlockSpec(shape, lambda bi, i: tuple(0 for _ in shape))
    tok = lambda w, cb: pl.BlockSpec((1, tm, w), lambda bi, i: (bi, i, cb))
    nq = PEER_HEADS * PEER_DK
    kv_w = MEM_HEADS * MEM_DH
    return pl.pallas_call(
        _merge_kernel,
        grid=(b, s // tm),
        in_specs=[tok(d, 0), tok(d, 0), tok(MLA_HEADS * MLA_V, 0),
                  tok(d, C_GATE // d), tok(d, C_GATE // d + 1), tok(d, C_GATE // d + 2),
                  tok(kv_w, C_MQ // kv_w),
                  pl.BlockSpec((1, MEM_TOKENS, 2 * kv_w), lambda bi, i: (bi, 0, 0)),
                  full((d, d)), full((MLA_HEADS * MLA_V, d)), full((kv_w, d)), full((d, d)),
                  full((d, nq)), full((1, MEM_DH)), full((1, MEM_DH)), full((1, d))],
        out_specs=[tok(d, 0), tok(d, 0), tok(nq, 0)],
        out_shape=[jax.ShapeDtypeStruct((b, s, d), F32), jax.ShapeDtypeStruct((b, s, d), F32),
                   jax.ShapeDtypeStruct((b, s, nq), F32)],
        compiler_params=_params(("parallel", "parallel"), 52),
        name="merge",
    )(x, o_gdn, o_mla, p3, p3, p3, p3, mkv, wts["w_branch_gdn"], wts["w_branch_mla"],
      wts["w_branch_mem"], wts["w_out"], wts["peer_w_q"], wts["mem_qk_norm_q"],
      wts["mem_qk_norm_k"], wts["g_ffn"])


def _top16(cur, order_key=None, payload=None):
    n, t = cur.shape
    if order_key is None:
        order_key = lax.broadcasted_iota(I32, (n, t), 0)
    slot = lax.broadcasted_iota(I32, (PEER_TOPK, t), 0)
    vals = jnp.zeros((PEER_TOPK, t), F32)
    picks = jnp.zeros((PEER_TOPK, t), I32)
    for r in range(PEER_TOPK):
        m = jnp.max(cur, axis=0, keepdims=True)
        am = jnp.min(jnp.where(cur == m, order_key, jnp.int32(2 ** 30)), axis=0, keepdims=True)
        hit = order_key == am
        pick = am if payload is None else jnp.max(jnp.where(hit, payload, -1), axis=0, keepdims=True)
        vals = jnp.where(slot == r, m, vals)
        picks = jnp.where(slot == r, pick, picks)
        cur = jnp.where(hit, -jnp.inf, cur)
    return vals, picks


def _staircase(x1, x2):
    lo1, hi1, lo2, hi2 = x1[0:8], x1[8:16], x2[0:8], x2[8:16]
    return jnp.concatenate(
        [x1[0:1] + lo2, x1[0:1] + hi2, x1[1:2] + lo2, x1[2:3] + lo2, x1[3:4] + lo2, x1[4:5] + lo2,
         hi1 + x2[0:1], lo1 + x2[0:1], lo1 + x2[1:2]], axis=0)


def _staircase_layout(t):
    k = lax.broadcasted_iota(I32, (8, t), 0)
    yes = k >= 0
    pos = jnp.concatenate([k, 8 + k, 16 + k, 32 + k, 48 + k, 64 + k, (8 + k) * 16, k * 16, k * 16 + 1],
                          axis=0)
    valid = jnp.concatenate([yes, yes, yes, k < 5, k < 4, k < 3, yes, k >= 5, k >= 5], axis=0)
    return pos, valid


def _peer_topk_kernel(q_ref, ka_ref, kb_ref, eidx_ref, gate_ref):
    half = PEER_DK // 2
    pos, valid = _staircase_layout(q_ref.shape[0])
    for h in range(PEER_HEADS):
        q1 = q_ref[:, h * PEER_DK:h * PEER_DK + half]
        q2 = q_ref[:, h * PEER_DK + half:(h + 1) * PEER_DK]
        s1 = _mm_nt(ka_ref[h], q1)
        s2 = _mm_nt(kb_ref[h], q2)
        v1, i1 = _top16(s1)
        v2, i2 = _top16(s2)
        cand = jnp.where(valid, _staircase(v1, v2), -jnp.inf)
        cidx = _staircase(i1 * PEER_KEYS, i2)
        top_s, eidx = _top16(cand, pos, cidx)
        p = jnp.exp(top_s - top_s[0:1, :])
        rows = slice(h * PEER_TOPK, (h + 1) * PEER_TOPK)
        gate_ref[0, rows, :] = p / jnp.sum(p, axis=0, keepdims=True)
        eidx_ref[0, rows, :] = eidx


def _peer_topk(qq, keys_a, keys_b):
    t, nq = qq.shape
    tp = PEER_TILE
    nt = t // tp
    kspec = pl.BlockSpec((PEER_HEADS, PEER_KEYS, PEER_DK // 2), lambda i: (0, 0, 0))
    return pl.pallas_call(
        _peer_topk_kernel,
        grid=(nt,),
        in_specs=[pl.BlockSpec((tp, nq), lambda i: (i, 0)), kspec, kspec],
        out_specs=[pl.BlockSpec((1, PEER_ROWS, tp), lambda i: (i, 0, 0)),
                   pl.BlockSpec((1, PEER_ROWS, tp), lambda i: (i, 0, 0))],
        out_shape=[jax.ShapeDtypeStruct((nt, PEER_ROWS, tp), I32),
                   jax.ShapeDtypeStruct((nt, PEER_ROWS, tp), F32)],
        compiler_params=_params(("parallel",), 32),
        name="peer_topk",
    )(qq, keys_a, keys_b)


def _peer_mix_kernel(eidx_hbm, uv_flat, uv_rows, gate_ref, zn_ref, x1_ref, o_ref, idx_smem, *rest):
    bufs = rest[:PEER_BUFS]
    sem, isem = rest[PEER_BUFS:]
    tp = zn_ref.shape[0]
    half = D_MODEL // 2
    n_idx = PEER_ROWS * tp
    step = pl.program_id(0)
    cur = step % 2

    def idx_copy(s, b):
        dst = idx_smem.at[pl.ds(pl.multiple_of(b * n_idx, n_idx), n_idx)]
        return pltpu.make_async_copy(eidx_hbm.at[s], dst, isem.at[b])

    has_next = step + 1 < pl.num_programs(0)
    ahead = PEER_BUFS - 1
    idx_base = cur * n_idx
    gate_t = gate_ref[0]

    def issue(t, slot, base=idx_base):
        off = base + t
        for j in range(PEER_ROWS):
            e = idx_smem[off + j * tp]
            src = uv_flat.at[pl.ds(pl.multiple_of(e * D_MODEL, D_MODEL), D_MODEL)]
            pltpu.make_async_copy(src, bufs[slot].at[j], sem.at[slot]).start()

    def wait(slot):
        pltpu.make_async_copy(uv_rows.at[pl.ds(0, PEER_ROWS)], bufs[slot], sem.at[slot]).wait()

    def unpack(words):
        lo = lax.bitcast_convert_type(words << 16, F32)
        hi = lax.bitcast_convert_type(words & jnp.int32(-65536), F32)
        return lo, hi

    def compute(t, slot):
        x = bufs[slot][...]
        z = zn_ref[pl.ds(t, 1), :]
        u_lo, u_hi = unpack(x[:, :half])
        prod = u_lo * z[:, :half] + u_hi * z[:, half:]
        part = prod[:, 0:LANES]
        for cch in range(1, half // LANES):
            part = part + prod[:, cch * LANES:(cch + 1) * LANES]
        dots = jnp.sum(part, axis=1, keepdims=True)
        act = 0.5 * dots * (1.0 + lax.erf(dots * (2.0 ** -0.5)))
        g_col = pltpu.roll(gate_t, (tp - t) % tp, 1)[:, 0:1]
        wgt = g_col * act
        v_lo, v_hi = unpack(x[:, half:])
        mix = jnp.concatenate([jnp.sum(wgt * v_lo, axis=0, keepdims=True),
                               jnp.sum(wgt * v_hi, axis=0, keepdims=True)], axis=1)
        o_ref[pl.ds(t, 1), :] = x1_ref[pl.ds(t, 1), :] + mix

    @pl.when(step == 0)
    def _():
        idx_copy(0, 0).start()
        idx_copy(0, 0).wait()
        for t in range(ahead):
            issue(t, t, 0)

    @pl.when(has_next)
    def _():
        idx_copy(step + 1, 1 - cur).start()

    def group(g, last):
        for s in range(PEER_BUFS):
            t = g * PEER_BUFS + s
            wait(s)
            if not last or s == 0:
                issue(t + ahead, (s + ahead) % PEER_BUFS)
            compute(t, s)
            if last and s < ahead:
                @pl.when(has_next)
                def _():
                    if s == 0:
                        idx_copy(step + 1, 1 - cur).wait()
                    issue(s, s, (1 - cur) * n_idx)

    n_groups = tp // PEER_BUFS

    def body(g, carry):
        group(g, False)
        return carry

    lax.fori_loop(0, n_groups - 1, body, 0)
    group(n_groups - 1, True)


def _peer_mix(eidx, gate, zn, x1, uv):
    t, d = zn.shape
    tp = PEER_TILE
    nt = t // tp
    return pl.pallas_call(
        _peer_mix_kernel,
        grid=(nt,),
        in_specs=[pl.BlockSpec(memory_space=pl.ANY),
                  pl.BlockSpec(memory_space=pl.ANY),
                  pl.BlockSpec(memory_space=pl.ANY),
                  pl.BlockSpec((1, PEER_ROWS, tp), lambda i: (i, 0, 0)),
                  pl.BlockSpec((tp, d), lambda i: (i, 0)),
                  pl.BlockSpec((tp, d), lambda i: (i, 0))],
        out_specs=pl.BlockSpec((tp, d), lambda i: (i, 0)),
        out_shape=jax.ShapeDtypeStruct((t, d), F32),
        scratch_shapes=([pltpu.SMEM((2 * PEER_ROWS * tp,), I32)]
                        + [pltpu.VMEM((PEER_ROWS, d), I32) for _ in range(PEER_BUFS)]
                        + [pltpu.SemaphoreType.DMA((PEER_BUFS,)), pltpu.SemaphoreType.DMA((2,))]),
        compiler_params=_params(("arbitrary",), 32),
        name="peer_mix",
    )(eidx.reshape(nt, PEER_ROWS * tp), uv.reshape(-1), uv, gate, zn, x1)


def _pack_pairs(tab):
    bits = lax.bitcast_convert_type(tab.astype(BF16), jnp.uint16).astype(jnp.uint32)
    half = tab.shape[1] // 2
    return lax.bitcast_convert_type(bits[:, :half] | (bits[:, half:] << 16), I32)


def _prepare(w_in, gdn_a_log_fwd, gdn_a_log_bwd, gdn_dt_bias_fwd, gdn_dt_bias_bwd, mla_w_uq, mla_w_ukv,
             mla_qk_norm_q, mla_qk_norm_k, peer_u, peer_v, **others):
    o = 0
    parts = {}
    for name, width in (("qkvz", 4096), ("small", 32), ("cq", MLA_Q_LORA), ("ckv", MLA_KV_LORA),
                        ("kpe", MLA_ROPE), ("mq", MEM_HEADS * MEM_DH), ("gate", 3 * D_MODEL)):
        parts[name] = w_in[:, o:o + width]
        o += width
    wts = dict(others)
    wts["w_main"] = jnp.concatenate(
        [parts["qkvz"], parts["gate"], parts["mq"], parts["cq"], parts["ckv"]], axis=1).astype(BF16)
    wts["w_small"] = jnp.concatenate(
        [parts["small"], parts["kpe"], jnp.zeros((D_MODEL, LANES - 32 - MLA_ROPE), F32)],
        axis=1).astype(BF16)
    zeros16 = jnp.zeros((LANES - 16,), F32)
    wts["a_lane"] = jnp.concatenate(
        [jnp.exp(gdn_a_log_fwd.astype(F32)), jnp.exp(gdn_a_log_bwd.astype(F32)), zeros16]).reshape(1, LANES)
    wts["dtb_lane"] = jnp.concatenate(
        [gdn_dt_bias_fwd.astype(F32), gdn_dt_bias_bwd.astype(F32), zeros16]).reshape(1, LANES)
    pad_q = jnp.zeros((MLA_Q_LORA, MLA_HEADS, LANES - MLA_QK), F32)
    wts["w_uq"] = jnp.concatenate(
        [mla_w_uq.reshape(MLA_Q_LORA, MLA_HEADS, MLA_QK), pad_q], axis=2
    ).reshape(MLA_Q_LORA, MLA_HEADS * LANES).astype(BF16)
    ukv = mla_w_ukv.reshape(MLA_KV_LORA, MLA_HEADS, MLA_NOPE + MLA_V)
    pad_k = jnp.zeros((MLA_KV_LORA, MLA_HEADS, LANES - MLA_NOPE), F32)
    wts["w_uk"] = jnp.concatenate([ukv[:, :, :MLA_NOPE], pad_k], axis=2).reshape(
        MLA_KV_LORA, MLA_HEADS * LANES).astype(BF16)
    wts["w_uv"] = ukv[:, :, MLA_NOPE:].reshape(MLA_KV_LORA, MLA_HEADS * MLA_V).astype(BF16)
    pad_n = jnp.zeros((LANES - MLA_QK,), F32)
    wts["qk_norm_q"] = jnp.concatenate([mla_qk_norm_q, pad_n]).reshape(1, LANES)
    wts["qk_norm_k"] = jnp.concatenate([mla_qk_norm_k, pad_n]).reshape(1, LANES)
    for name in ("mla_q_norm", "mla_kv_norm", "mem_qk_norm_q", "mem_qk_norm_k", "g_ffn"):
        wts[name] = wts[name].reshape(1, -1)
    for name in ("w_branch_gdn", "w_branch_mla", "w_branch_mem", "w_out", "peer_w_q", "mem_w_kv",
                 "peer_keys_a", "peer_keys_b"):
        wts[name] = wts[name].astype(BF16)
    wts["peer_uv"] = jnp.concatenate([_pack_pairs(peer_u), _pack_pairs(peer_v)], axis=1)
    return wts


def _rope_tables(s):
    half = MLA_ROPE // 2
    inv = ROPE_THETA ** (-jnp.arange(half, dtype=F32) / half)
    ang = jnp.arange(s).astype(F32)[:, None] * inv[None, :]
    cos = jnp.cos(ang)
    sin = jnp.sin(ang)
    ones = jnp.ones((s, MLA_NOPE), F32)
    pad = jnp.zeros((s, LANES - MLA_QK), F32)
    zeros_n = jnp.zeros((s, MLA_NOPE), F32)
    zeros_h = jnp.zeros((s, half), F32)
    cos_t = jnp.concatenate([ones, cos, cos, pad], axis=1)
    sin_a = jnp.concatenate([zeros_n, zeros_h, sin, pad], axis=1)
    sin_b = jnp.concatenate([zeros_n, -sin, zeros_h, pad], axis=1)
    return cos_t, sin_a, sin_b


def _layer(x, mem, wts):
    b, s, d = x.shape
    t = b * s
    p_main, p_small = _norm_proj(x.reshape(t, d), wts["g_mix"], wts["w_main"], wts["w_small"], 5, BF16)
    p3 = p_main.reshape(b, s, N_MAIN)
    ps3 = p_small.reshape(b, s, LANES)
    qkv = _gdn_conv(p3, wts["gdn_conv_w"])
    o_fwd = _gdn_scan(qkv, ps3, wts["a_lane"], wts["dtb_lane"], False)
    o_gdn = _gdn_scan(qkv, ps3, wts["a_lane"], wts["dtb_lane"], True, o_fwd, p3, wts["gdn_out_norm"])
    q_m, k_m, v_m = _mla_prep(p3, ps3, _rope_tables(s), wts)
    o_mla = _attention(q_m, k_m, v_m)
    (mkv,) = _norm_proj(mem.reshape(b * MEM_TOKENS, d), wts["mem_norm"], wts["mem_w_kv"], None, 1, F32)
    mkv = mkv.reshape(b, MEM_TOKENS, 2 * MEM_HEADS * MEM_DH)
    x1, zn, qq = _merge(x, o_gdn, o_mla, p3, mkv, wts)
    eidx, gate = _peer_topk(qq.reshape(t, -1), wts["peer_keys_a"], wts["peer_keys_b"])
    y = _peer_mix(eidx, gate, zn.reshape(t, d), x1.reshape(t, d), wts["peer_uv"])
    return y.reshape(b, s, d)


def kernel(x_prompt, x_sample, mem_prompt, mem_sample, g_mix, w_in, gdn_conv_w, gdn_a_log_fwd,
           gdn_a_log_bwd, gdn_dt_bias_fwd, gdn_dt_bias_bwd, gdn_out_norm, mla_q_norm, mla_w_uq,
           mla_kv_norm, mla_w_ukv, mla_qk_norm_q, mla_qk_norm_k, mem_norm, mem_w_kv, mem_qk_norm_q,
           mem_qk_norm_k, w_branch_gdn, w_branch_mla, w_branch_mem, w_out, g_ffn, peer_w_q,
           peer_keys_a, peer_keys_b, peer_u, peer_v):
    wts = _prepare(
        w_in, gdn_a_log_fwd, gdn_a_log_bwd, gdn_dt_bias_fwd, gdn_dt_bias_bwd, mla_w_uq, mla_w_ukv,
        mla_qk_norm_q, mla_qk_norm_k, peer_u, peer_v,
        g_mix=g_mix, gdn_conv_w=gdn_conv_w, gdn_out_norm=gdn_out_norm, mla_q_norm=mla_q_norm,
        mla_kv_norm=mla_kv_norm, mem_norm=mem_norm, mem_w_kv=mem_w_kv, mem_qk_norm_q=mem_qk_norm_q,
        mem_qk_norm_k=mem_qk_norm_k, w_branch_gdn=w_branch_gdn, w_branch_mla=w_branch_mla,
        w_branch_mem=w_branch_mem, w_out=w_out, g_ffn=g_ffn, peer_w_q=peer_w_q,
        peer_keys_a=peer_keys_a, peer_keys_b=peer_keys_b)
    return (_layer(x_prompt, mem_prompt, wts), _layer(x_sample, mem_sample, wts))
```

```python
import functools

import jax
import jax.numpy as jnp
from jax import lax
from jax.experimental import pallas as pl
from jax.experimental.pallas import tpu as pltpu

F32 = jnp.float32
BF16 = jnp.bfloat16
I32 = jnp.int32

D_MODEL = 1024
GDN_HEADS = 8
GDN_DK = 128
GDN_DV = 128
GDN_CONV = 5
GDN_CHUNK = 64
MLA_HEADS = 8
MLA_NOPE = 64
MLA_ROPE = 32
MLA_QK = MLA_NOPE + MLA_ROPE
MLA_V = 64
MLA_Q_LORA = 384
MLA_KV_LORA = 256
ROPE_THETA = 10000.0
MEM_TOKENS = 256
MEM_HEADS = 4
MEM_DH = 128
PEER_HEADS = 8
PEER_DK = 256
PEER_KEYS = 128
PEER_TOPK = 16
NORM_EPS = 1e-6
LOG2E = 1.4426950408889634

LANES = 128
MIB = 1024 * 1024

C_Q, C_K, C_V, C_Z = 0, 1024, 2048, 3072
C_GATE = 4096
C_MQ = 7168
C_LAT = 7680
N_MAIN = 8320
L_AF, L_AB, L_BF, L_BB, L_KPE = 0, 8, 16, 24, 32

PEER_TILE = 128
PEER_ROWS = PEER_HEADS * PEER_TOPK
PEER_BUFS = 8


def _params(sem, vmem_mib):
    return pltpu.CompilerParams(dimension_semantics=sem, vmem_limit_bytes=vmem_mib * MIB)


def _mm(a, b):
    return jnp.dot(a.astype(BF16), b.astype(BF16), preferred_element_type=F32)


def _mm_nt(a, b):
    return lax.dot_general(a.astype(BF16), b.astype(BF16), (((1,), (1,)), ((), ())),
                           preferred_element_type=F32)


def _mm_tn(a, b):
    return lax.dot_general(a.astype(BF16), b.astype(BF16), (((0,), (0,)), ((), ())),
                           preferred_element_type=F32)


def _rms(x, n=None):
    n = x.shape[-1] if n is None else n
    return x * lax.rsqrt(jnp.sum(x * x, axis=-1, keepdims=True) * (1.0 / n) + NORM_EPS)


def _norm_proj_kernel(has_small, x_ref, g_ref, w_ref, *rest):
    if has_small:
        ws_ref, o_ref, os_ref, h_scr = rest
    else:
        o_ref, h_scr = rest

    @pl.when(pl.program_id(1) == 0)
    def _():
        h = _rms(x_ref[...]) * g_ref[...]
        h_scr[...] = h.astype(BF16)
        if has_small:
            os_ref[...] = jnp.dot(h_scr[...], ws_ref[...], preferred_element_type=F32)

    o_ref[...] = jnp.dot(h_scr[...], w_ref[...], preferred_element_type=F32).astype(o_ref.dtype)


def _norm_proj(x2d, g, w, w_small, n_col_blocks, out_dtype):
    t, d = x2d.shape
    n = w.shape[1]
    tm = min(1024, t)
    tn = n // n_col_blocks
    in_specs = [pl.BlockSpec((tm, d), lambda i, j: (i, 0)),
                pl.BlockSpec((1, d), lambda i, j: (0, 0)),
                pl.BlockSpec((d, tn), lambda i, j: (0, j))]
    out_shape = [jax.ShapeDtypeStruct((t, n), out_dtype)]
    out_specs = [pl.BlockSpec((tm, tn), lambda i, j: (i, j))]
    args = [x2d, g.reshape(1, d), w]
    if w_small is not None:
        in_specs.append(pl.BlockSpec((d, LANES), lambda i, j: (0, 0)))
        out_shape.append(jax.ShapeDtypeStruct((t, LANES), F32))
        out_specs.append(pl.BlockSpec((tm, LANES), lambda i, j: (i, 0)))
        args.append(w_small)
    return pl.pallas_call(
        functools.partial(_norm_proj_kernel, w_small is not None),
        grid=(t // tm, n_col_blocks),
        in_specs=in_specs, out_specs=out_specs, out_shape=out_shape,
        scratch_shapes=[pltpu.VMEM((tm, d), BF16)],
        compiler_params=_params(("parallel", "arbitrary"), 48),
        name="norm_proj",
    )(*args)


def _gdn_conv_kernel(ts, x_ref, w_ref, o_ref):
    s = x_ref.shape[1]
    n_chunks = s // ts
    halo = 16
    n_ext = ts + 2 * halo
    is_qk = pl.program_id(1) < 2 * GDN_HEADS
    w = w_ref[...]

    def body(i, carry):
        t0 = pl.multiple_of(i * ts, ts)
        main = x_ref[0, pl.ds(t0, ts), :].astype(F32)
        p0 = pl.multiple_of(jnp.maximum(t0 - halo, 0), halo)
        n0 = pl.multiple_of(jnp.minimum(t0 + ts, s - halo), halo)
        prev = jnp.where(i > 0, x_ref[0, pl.ds(p0, halo), :].astype(F32), 0.0)
        nxt = jnp.where(i < n_chunks - 1, x_ref[0, pl.ds(n0, halo), :].astype(F32), 0.0)
        ext = jnp.concatenate([prev, main, nxt], axis=0)
        acc = w[2:3, :] * ext
        for j in (0, 1, 3, 4):
            acc = acc + w[j:j + 1, :] * pltpu.roll(ext, (2 - j) % n_ext, 0)
        y = acc[halo:halo + ts]
        y = y * jax.nn.sigmoid(y)
        y_n = y * lax.rsqrt(jnp.sum(y * y, axis=-1, keepdims=True) + NORM_EPS)
        o_ref[0, 0, pl.ds(t0, ts), :] = jnp.where(is_qk, y_n, y).astype(o_ref.dtype)
        return carry

    lax.fori_loop(0, n_chunks, body, 0)


def _gdn_conv(p3, conv_w):
    b, s, _ = p3.shape
    ts = min(512, s)
    n_blk = 3 * GDN_HEADS
    return pl.pallas_call(
        functools.partial(_gdn_conv_kernel, ts),
        grid=(b, n_blk),
        in_specs=[pl.BlockSpec((1, s, LANES), lambda bi, c: (bi, 0, c)),
                  pl.BlockSpec((GDN_CONV, LANES), lambda bi, c: (0, c))],
        out_specs=pl.BlockSpec((1, 1, s, LANES), lambda bi, c: (bi, c, 0, 0)),
        out_shape=jax.ShapeDtypeStruct((b, n_blk, s, LANES), BF16),
        compiler_params=_params(("parallel", "parallel"), 40),
        name="gdn_conv",
    )(p3, conv_w)


def _gdn_scan_kernel(reverse, hb, ts, *refs):
    if reverse:
        q_ref, k_ref, v_ref, ps_ref, al_ref, dtb_ref, of_ref, z_ref, gn_ref, o_ref, s_scr = refs
    else:
        q_ref, k_ref, v_ref, ps_ref, al_ref, dtb_ref, o_ref, s_scr = refs
    c = GDN_CHUNK
    n_chunks = ts // c

    @pl.when(pl.program_id(2) == 0)
    def _():
        s_scr[...] = jnp.zeros_like(s_scr)

    ps = ps_ref[0]
    g = -al_ref[...] * jax.nn.softplus(ps + dtb_ref[...])
    beta = jax.nn.sigmoid(ps)
    rowid = lax.broadcasted_iota(I32, (ts, LANES), 0) & (c - 1)
    gc = g
    for kk in (1, 2, 4, 8, 16, 32):
        if reverse:
            gc = gc + jnp.where(rowid < c - kk, pltpu.roll(gc, ts - kk, 0), 0.0)
        else:
            gc = gc + jnp.where(rowid >= kk, pltpu.roll(gc, kk, 0), 0.0)
    shift = (LANES - pl.program_id(1) * hb) % LANES
    gc = pltpu.roll(gc, shift, 1)
    beta = pltpu.roll(beta, shift, 1)
    eg = jnp.exp(gc)
    gc_t = [gc[p * LANES:(p + 1) * LANES, :].T for p in range(ts // LANES)]

    r = lax.broadcasted_iota(I32, (c, c), 0)
    cc = lax.broadcasted_iota(I32, (c, c), 1)
    if reverse:
        r, cc = cc, r
    incl = r >= cc
    strict = r > cc
    eye = jnp.where(r == cc, 1.0, 0.0)
    level_f = [jnp.where((((r >> s) & 1) == 1) & ((cc >> s) == (r >> s) - 1), 1.0, 0.0)
               for s in range(6)]

    lane_g = L_AB if reverse else L_AF
    lane_b = L_BB if reverse else L_BF
    scale = GDN_DK ** -0.5
    order = list(range(n_chunks - 1, -1, -1)) if reverse else list(range(n_chunks))
    units = [(ci, hh) for ci in order for hh in range(hb)]

    kb16, k16, rhs, dec, kd16, a2q, gl = {}, {}, {}, {}, {}, {}, {}
    for un in units:
        ci, hh = un
        r0 = ci * c
        q = q_ref[0, hh, r0:r0 + c, :].astype(F32) * scale
        k = k_ref[0, hh, r0:r0 + c, :].astype(F32)
        v = v_ref[0, hh, r0:r0 + c, :].astype(F32)
        la = lane_g + hh
        lb = lane_b + hh
        gc_c = gc[r0:r0 + c, la:la + 1]
        eg_c = eg[r0:r0 + c, la:la + 1]
        b_c = beta[r0:r0 + c, lb:lb + 1]
        gc_r = gc_t[r0 // LANES][la:la + 1, r0 % LANES:r0 % LANES + c]
        r_last = r0 if reverse else r0 + c - 1
        gc_l = gc[r_last:r_last + 1, la:la + 1]
        dec[un] = jnp.exp(jnp.where(incl, gc_c - gc_r, -1e30))
        kb = k * b_c
        kb16[un] = jnp.concatenate([kb, q], axis=0).astype(BF16)
        k16[un] = k.astype(BF16)
        rhs[un] = jnp.concatenate([v * b_c, kb * eg_c], axis=1).astype(BF16)
        a2q[un] = (q * eg_c).astype(BF16)
        kd16[un] = (k * jnp.exp(gc_l - gc_c)).astype(BF16)
        gl[un] = jnp.exp(gc_l)
    kq = {un: lax.dot_general(kb16[un], k16[un], (((1,), (1,)), ((), ())),
                              preferred_element_type=F32) for un in units}
    lm = {un: jnp.where(strict, kq[un][:c] * dec[un], 0.0) for un in units}
    intra = {un: jnp.where(incl, kq[un][c:] * dec[un], 0.0).astype(BF16) for un in units}
    dm = {un: eye - lm[un] * level_f[0] for un in units}
    for s in range(1, 6):
        cd = {un: _mm(lm[un] * level_f[s], dm[un]) for un in units}
        dm = {un: dm[un] - _mm(dm[un], cd[un]) for un in units}
    sol = {un: _mm(dm[un], rhs[un]) for un in units}

    st = [s_scr[hh] for hh in range(hb)]
    for ci in order:
        r0 = ci * c
        heads = [(ci, hh) for hh in range(hb)]
        a2 = [jnp.concatenate([sol[un][:, GDN_DV:].astype(BF16), a2q[un]], axis=0) for un in heads]
        ws_qs = [jnp.dot(a2[hh], st[hh].astype(BF16), preferred_element_type=F32) for hh in range(hb)]
        v_new = [sol[un][:, :GDN_DV] - ws_qs[un[1]][:c] for un in heads]
        v16 = [vn.astype(BF16) for vn in v_new]
        o = [ws_qs[hh][c:] + jnp.dot(intra[(ci, hh)], v16[hh], preferred_element_type=F32)
             for hh in range(hb)]
        st = [st[hh] * gl[(ci, hh)]
              + lax.dot_general(kd16[(ci, hh)], v16[hh], (((0,), (0,)), ((), ())),
                                preferred_element_type=F32) for hh in range(hb)]
        for hh in range(hb):
            cols = slice(hh * GDN_DV, (hh + 1) * GDN_DV)
            if reverse:
                tot = o[hh] + of_ref[0, r0:r0 + c, cols]
                zz = z_ref[0, r0:r0 + c, cols].astype(F32)
                y = _rms(tot) * gn_ref[...] * (zz * jax.nn.sigmoid(zz))
                o_ref[0, r0:r0 + c, cols] = y.astype(o_ref.dtype)
            else:
                o_ref[0, r0:r0 + c, cols] = o[hh]
    for hh in range(hb):
        s_scr[hh] = st[hh]


def _gdn_scan(qkv, ps3, a_lane, dtb_lane, reverse, o_fwd=None, p3=None, out_norm=None):
    b, _, s, _ = qkv.shape
    hb = 8
    ts = 256
    nt = s // ts
    n_hblk = GDN_HEADS // hb
    tile = (lambda i: nt - 1 - i) if reverse else (lambda i: i)
    in_specs = [
        pl.BlockSpec((1, hb, ts, LANES), lambda bi, h, i: (bi, h, tile(i), 0)),
        pl.BlockSpec((1, hb, ts, LANES), lambda bi, h, i: (bi, n_hblk + h, tile(i), 0)),
        pl.BlockSpec((1, hb, ts, LANES), lambda bi, h, i: (bi, 2 * n_hblk + h, tile(i), 0)),
        pl.BlockSpec((1, ts, LANES), lambda bi, h, i: (bi, tile(i), 0)),
        pl.BlockSpec((1, LANES), lambda bi, h, i: (0, 0)),
        pl.BlockSpec((1, LANES), lambda bi, h, i: (0, 0)),
    ]
    args = [qkv, qkv, qkv, ps3, a_lane, dtb_lane]
    wide = hb * GDN_DV
    if reverse:
        in_specs += [
            pl.BlockSpec((1, ts, wide), lambda bi, h, i: (bi, tile(i), h)),
            pl.BlockSpec((1, ts, wide), lambda bi, h, i: (bi, tile(i), C_Z // wide + h)),
            pl.BlockSpec((1, GDN_DV), lambda bi, h, i: (0, 0)),
        ]
        args += [o_fwd, p3, out_norm.reshape(1, GDN_DV)]
    out_dtype = BF16 if reverse else F32
    return pl.pallas_call(
        functools.partial(_gdn_scan_kernel, reverse, hb, ts),
        grid=(b, n_hblk, nt),
        in_specs=in_specs,
        out_specs=pl.BlockSpec((1, ts, wide), lambda bi, h, i: (bi, tile(i), h)),
        out_shape=jax.ShapeDtypeStruct((b, s, GDN_HEADS * GDN_DV), out_dtype),
        scratch_shapes=[pltpu.VMEM((hb, GDN_DK, GDN_DV), F32)],
        compiler_params=_params(("parallel", "parallel", "arbitrary"), 32),
        name="gdn_scan_bwd" if reverse else "gdn_scan_fwd",
    )(*args)


def _mla_prep_kernel(lat_ref, ps_ref, cos_ref, sa_ref, sb_ref, gq_ref, gkv_ref, wuq_ref, wuk_ref,
                     wuv_ref, nq_ref, nk_ref, q_ref, k_ref, v_ref):
    lat = lat_ref[0].astype(F32)
    cq = _rms(lat[:, :MLA_Q_LORA]) * gq_ref[...]
    ckv = _rms(lat[:, MLA_Q_LORA:]) * gkv_ref[...]
    q_all = _mm(cq, wuq_ref[...])
    k_all = _mm(ckv, wuk_ref[...])
    v_ref[0] = _mm(ckv, wuv_ref[...]).astype(v_ref.dtype)
    ps = ps_ref[0]
    lane = lax.broadcasted_iota(I32, ps.shape, 1)
    in_rope = (lane >= MLA_NOPE) & (lane < MLA_QK)
    kpe = jnp.where(in_rope, pltpu.roll(ps, MLA_NOPE - L_KPE, 1), 0.0)
    cos = cos_ref[...]
    sin_a = sa_ref[...]
    sin_b = sb_ref[...]
    half = MLA_ROPE // 2

    def norm_rope(x, gain):
        y = _rms(x, MLA_QK) * gain
        return y * cos + pltpu.roll(y, half, 1) * sin_a + pltpu.roll(y, LANES - half, 1) * sin_b

    for h in range(MLA_HEADS):
        cols = slice(h * LANES, (h + 1) * LANES)
        q_ref[0, h] = (norm_rope(q_all[:, cols], nq_ref[...]) * (MLA_QK ** -0.5 * LOG2E)).astype(q_ref.dtype)
        k_ref[0, h] = norm_rope(k_all[:, cols] + kpe, nk_ref[...]).astype(k_ref.dtype)


def _mla_prep(p3, ps3, rope_tabs, wts):
    b, s, _ = p3.shape
    tm = min(512, s)
    lat_w = MLA_Q_LORA + MLA_KV_LORA
    full = lambda shape: pl.BlockSpec(shape, lambda bi, i: tuple(0 for _ in shape))
    tab = pl.BlockSpec((tm, LANES), lambda bi, i: (i, 0))
    hw = MLA_HEADS * LANES
    return pl.pallas_call(
        _mla_prep_kernel,
        grid=(b, s // tm),
        in_specs=[pl.BlockSpec((1, tm, lat_w), lambda bi, i: (bi, i, C_LAT // lat_w)),
                  pl.BlockSpec((1, tm, LANES), lambda bi, i: (bi, i, 0)),
                  tab, tab, tab,
                  full((1, MLA_Q_LORA)), full((1, MLA_KV_LORA)),
                  full((MLA_Q_LORA, hw)), full((MLA_KV_LORA, hw)),
                  full((MLA_KV_LORA, MLA_HEADS * MLA_V)),
                  full((1, LANES)), full((1, LANES))],
        out_specs=[pl.BlockSpec((1, MLA_HEADS, tm, LANES), lambda bi, i: (bi, 0, i, 0)),
                   pl.BlockSpec((1, MLA_HEADS, tm, LANES), lambda bi, i: (bi, 0, i, 0)),
                   pl.BlockSpec((1, tm, MLA_HEADS * MLA_V), lambda bi, i: (bi, i, 0))],
        out_shape=[jax.ShapeDtypeStruct((b, MLA_HEADS, s, LANES), BF16),
                   jax.ShapeDtypeStruct((b, MLA_HEADS, s, LANES), BF16),
                   jax.ShapeDtypeStruct((b, s, MLA_HEADS * MLA_V), BF16)],
        compiler_params=_params(("parallel", "parallel"), 40),
        name="mla_prep",
    )(p3, ps3, *rope_tabs, wts["mla_q_norm"], wts["mla_kv_norm"], wts["w_uq"], wts["w_uk"],
      wts["w_uv"], wts["qk_norm_q"], wts["qk_norm_k"])


def _attn_kernel(tk_sub, q_ref, k_ref, v_ref, o_ref, m_scr, l_scr, acc_scr):
    j = pl.program_id(3)

    @pl.when(j == 0)
    def _():
        m_scr[...] = jnp.full_like(m_scr, -jnp.inf)
        l_scr[...] = jnp.zeros_like(l_scr)
        acc_scr[...] = jnp.zeros_like(acc_scr)

    n_sub = k_ref.shape[2] // tk_sub
    n_lt = tk_sub // LANES
    m = [m_scr[hh] for hh in range(2)]
    l = [l_scr[hh] for hh in range(2)]
    acc = [acc_scr[hh] for hh in range(2)]
    for sb in range(n_sub):
        rows = slice(sb * tk_sub, (sb + 1) * tk_sub)
        v = v_ref[0, rows, :]
        for hh in range(2):
            s = lax.dot_general(q_ref[0, hh], k_ref[0, hh, rows, :], (((1,), (1,)), ((), ())),
                                preferred_element_type=F32)
            tiles = [s[:, t * LANES:(t + 1) * LANES] for t in range(n_lt)]
            mx = tiles[0]
            for t in tiles[1:]:
                mx = jnp.maximum(mx, t)
            m_new = jnp.maximum(m[hh], jnp.max(mx, axis=-1, keepdims=True))
            alpha = jnp.exp2(m[hh] - m_new)
            ps = [jnp.exp2(t - m_new) for t in tiles]
            psum = ps[0]
            for t in ps[1:]:
                psum = psum + t
            l[hh] = alpha * l[hh] + psum
            p = jnp.concatenate(ps, axis=1).astype(BF16)
            acc[hh] = alpha * acc[hh] + jnp.dot(p, v, preferred_element_type=F32)
            m[hh] = m_new
    for hh in range(2):
        m_scr[hh] = m[hh]
        l_scr[hh] = l[hh]
        acc_scr[hh] = acc[hh]

    @pl.when(j == pl.num_programs(3) - 1)
    def _():
        lane = lax.broadcasted_iota(I32, o_ref.shape[1:], 1)
        o0 = acc[0] / jnp.sum(l[0], axis=-1, keepdims=True)
        o1 = acc[1] / jnp.sum(l[1], axis=-1, keepdims=True)
        o_ref[0] = jnp.where(lane < MLA_V, o0, o1).astype(o_ref.dtype)


def _attention(q, k, v):
    b, h, s, _ = q.shape
    tq = min(512, s)
    tk = min(4096, s)
    tk_sub = min(256, tk)
    return pl.pallas_call(
        functools.partial(_attn_kernel, tk_sub),
        grid=(b, h // 2, s // tq, s // tk),
        in_specs=[pl.BlockSpec((1, 2, tq, LANES), lambda bi, p, i, j: (bi, p, i, 0)),
                  pl.BlockSpec((1, 2, tk, LANES), lambda bi, p, i, j: (bi, p, j, 0)),
                  pl.BlockSpec((1, tk, LANES), lambda bi, p, i, j: (bi, j, p))],
        out_specs=pl.BlockSpec((1, tq, LANES), lambda bi, p, i, j: (bi, i, p)),
        out_shape=jax.ShapeDtypeStruct((b, s, h * MLA_V), BF16),
        scratch_shapes=[pltpu.VMEM((2, tq, LANES), F32), pltpu.VMEM((2, tq, LANES), F32),
                        pltpu.VMEM((2, tq, LANES), F32)],
        compiler_params=_params(("parallel", "parallel", "parallel", "arbitrary"), 40),
        name="mla_attention",
    )(q, k, v)


def _merge_kernel(x_ref, og_ref, om_ref, g0_ref, g1_ref, g2_ref, mq_ref, mkv_ref, wbg_ref, wbm_ref,
                  wbe_ref, wo_ref, wq_ref, nq_ref, nk_ref, gf_ref, x1_ref, zn_ref, qq_ref):
    mq = mq_ref[0].astype(F32)
    mkv = mkv_ref[0]
    kv_w = MEM_HEADS * MEM_DH
    outs = []
    for h in range(MEM_HEADS):
        cols = slice(h * MEM_DH, (h + 1) * MEM_DH)
        qh = _rms(mq[:, cols]) * nq_ref[...]
        kh = _rms(mkv[:, cols]) * nk_ref[...]
        vh = mkv[:, kv_w + h * MEM_DH:kv_w + (h + 1) * MEM_DH]
        s = _mm_nt(qh, kh) * (MEM_DH ** -0.5)
        p = jnp.exp(s - jnp.max(s, axis=-1, keepdims=True))
        p = p / jnp.sum(p, axis=-1, keepdims=True)
        outs.append(_mm(p, vh))
    y_mem = _mm(jnp.concatenate(outs, axis=1), wbe_ref[...])
    y_gdn = jnp.dot(og_ref[0], wbg_ref[...], preferred_element_type=F32)
    y_mla = jnp.dot(om_ref[0], wbm_ref[...], preferred_element_type=F32)
    merged = (jax.nn.sigmoid(g0_ref[0].astype(F32)) * y_gdn
              + jax.nn.sigmoid(g1_ref[0].astype(F32)) * y_mla
              + jax.nn.sigmoid(g2_ref[0].astype(F32)) * y_mem)
    x1 = x_ref[0] + _mm(merged, wo_ref[...])
    x1_ref[0] = x1
    zn = _rms(x1) * gf_ref[...]
    zn_ref[0] = zn
    qq_ref[0] = _mm(zn, wq_ref[...])


def _merge(x, o_gdn, o_mla, p3, mkv, wts):
    b, s, d = x.shape
    tm = min(256, s)
    full = lambda shape: pl.BlockSpec(shape, lambda bi, i: tuple(0 for _ in shape))
    tok = lambda w, cb: pl.BlockSpec((1, tm, w), lambda bi, i: (bi, i, cb))
    nq = PEER_HEADS * PEER_DK
    kv_w = MEM_HEADS * MEM_DH
    return pl.pallas_call(
        _merge_kernel,
        grid=(b, s // tm),
        in_specs=[tok(d, 0), tok(d, 0), tok(MLA_HEADS * MLA_V, 0),
                  tok(d, C_GATE // d), tok(d, C_GATE // d + 1), tok(d, C_GATE // d + 2),
                  tok(kv_w, C_MQ // kv_w),
                  pl.BlockSpec((1, MEM_TOKENS, 2 * kv_w), lambda bi, i: (bi, 0, 0)),
                  full((d, d)), full((MLA_HEADS * MLA_V, d)), full((kv_w, d)), full((d, d)),
                  full((d, nq)), full((1, MEM_DH)), full((1, MEM_DH)), full((1, d))],
        out_specs=[tok(d, 0), tok(d, 0), tok(nq, 0)],
        out_shape=[jax.ShapeDtypeStruct((b, s, d), F32), jax.ShapeDtypeStruct((b, s, d), F32),
                   jax.ShapeDtypeStruct((b, s, nq), F32)],
        compiler_params=_params(("parallel", "parallel"), 52),
        name="merge",
    )(x, o_gdn, o_mla, p3, p3, p3, p3, mkv, wts["w_branch_gdn"], wts["w_branch_mla"],
      wts["w_branch_mem"], wts["w_out"], wts["peer_w_q"], wts["mem_qk_norm_q"],
      wts["mem_qk_norm_k"], wts["g_ffn"])


def _top16(cur, order_key=None, payload=None):
    n, t = cur.shape
    if order_key is None:
        order_key = lax.broadcasted_iota(I32, (n, t), 0)
    order_key = order_key.astype(F32)
    payload = None if payload is None else payload.astype(F32)
    slot = lax.broadcasted_iota(I32, (PEER_TOPK, t), 0)
    vals = jnp.zeros((PEER_TOPK, t), F32)
    picks = jnp.zeros((PEER_TOPK, t), F32)
    for r in range(PEER_TOPK):
        m = jnp.max(cur, axis=0, keepdims=True)
        am = jnp.min(jnp.where(cur == m, order_key, 2.0 ** 30), axis=0, keepdims=True)
        hit = order_key == am
        pick = am if payload is None else jnp.max(jnp.where(hit, payload, -1.0), axis=0, keepdims=True)
        vals = jnp.where(slot == r, m, vals)
        picks = jnp.where(slot == r, pick, picks)
        cur = jnp.where(hit, -jnp.inf, cur)
    return vals, picks.astype(I32)


def _staircase(x1, x2):
    lo1, hi1, lo2, hi2 = x1[0:8], x1[8:16], x2[0:8], x2[8:16]
    return jnp.concatenate(
        [x1[0:1] + lo2, x1[0:1] + hi2, x1[1:2] + lo2, x1[2:3] + lo2, x1[3:4] + lo2, x1[4:5] + lo2,
         hi1 + x2[0:1], lo1 + x2[0:1], lo1 + x2[1:2]], axis=0)


def _staircase_layout(t):
    k = lax.broadcasted_iota(I32, (8, t), 0)
    yes = k >= 0
    pos = jnp.concatenate([k, 8 + k, 16 + k, 32 + k, 48 + k, 64 + k, (8 + k) * 16, k * 16, k * 16 + 1],
                          axis=0)
    valid = jnp.concatenate([yes, yes, yes, k < 5, k < 4, k < 3, yes, k >= 5, k >= 5], axis=0)
    return pos, valid


def _peer_topk_kernel(q_ref, ka_ref, kb_ref, eidx_ref, gate_ref):
    half = PEER_DK // 2
    pos, valid = _staircase_layout(q_ref.shape[0])
    for h in range(PEER_HEADS):
        q1 = q_ref[:, h * PEER_DK:h * PEER_DK + half]
        q2 = q_ref[:, h * PEER_DK + half:(h + 1) * PEER_DK]
        s1 = _mm_nt(ka_ref[h], q1)
        s2 = _mm_nt(kb_ref[h], q2)
        v1, i1 = _top16(s1)
        v2, i2 = _top16(s2)
        cand = jnp.where(valid, _staircase(v1, v2), -jnp.inf)
        cidx = _staircase(i1 * PEER_KEYS, i2)
        top_s, eidx = _top16(cand, pos, cidx)
        p = jnp.exp(top_s - top_s[0:1, :])
        rows = slice(h * PEER_TOPK, (h + 1) * PEER_TOPK)
        gate_ref[0, rows, :] = p / jnp.sum(p, axis=0, keepdims=True)
        eidx_ref[0, rows, :] = eidx


def _peer_topk(qq, keys_a, keys_b):
    t, nq = qq.shape
    tp = PEER_TILE
    nt = t // tp
    kspec = pl.BlockSpec((PEER_HEADS, PEER_KEYS, PEER_DK // 2), lambda i: (0, 0, 0))
    return pl.pallas_call(
        _peer_topk_kernel,
        grid=(nt,),
        in_specs=[pl.BlockSpec((tp, nq), lambda i: (i, 0)), kspec, kspec],
        out_specs=[pl.BlockSpec((1, PEER_ROWS, tp), lambda i: (i, 0, 0)),
                   pl.BlockSpec((1, PEER_ROWS, tp), lambda i: (i, 0, 0))],
        out_shape=[jax.ShapeDtypeStruct((nt, PEER_ROWS, tp), I32),
                   jax.ShapeDtypeStruct((nt, PEER_ROWS, tp), F32)],
        compiler_params=_params(("parallel",), 32),
        name="peer_topk",
    )(qq, keys_a, keys_b)


def _peer_mix_kernel(eidx_hbm, uv_flat, uv_rows, gate_ref, zn_ref, x1_ref, o_ref, idx_smem, *rest):
    bufs = rest[:PEER_BUFS]
    sem, isem = rest[PEER_BUFS:]
    tp = zn_ref.shape[0]
    half = D_MODEL // 2
    n_idx = PEER_ROWS * tp
    step = pl.program_id(0)
    cur = step % 2

    def idx_copy(s, b):
        dst = idx_smem.at[pl.ds(pl.multiple_of(b * n_idx, n_idx), n_idx)]
        return pltpu.make_async_copy(eidx_hbm.at[s], dst, isem.at[b])

    has_next = step + 1 < pl.num_programs(0)
    ahead = PEER_BUFS - 1
    idx_base = cur * n_idx
    gate_t = gate_ref[0]

    def issue(t, slot, base=idx_base):
        off = base + t
        for j in range(PEER_ROWS):
            e = idx_smem[off + j * tp]
            src = uv_flat.at[pl.ds(pl.multiple_of(e * D_MODEL, D_MODEL), D_MODEL)]
            pltpu.make_async_copy(src, bufs[slot].at[j], sem.at[slot]).start(priority=j % 2)

    def wait(slot):
        pltpu.make_async_copy(uv_rows.at[pl.ds(0, PEER_ROWS)], bufs[slot], sem.at[slot]).wait()

    def unpack(words):
        lo = lax.bitcast_convert_type(words << 16, F32)
        hi = lax.bitcast_convert_type(words & jnp.int32(-65536), F32)
        return lo, hi

    def compute(t, slot):
        x = bufs[slot][...]
        z = zn_ref[pl.ds(t, 1), :]
        u_lo, u_hi = unpack(x[:, :half])
        prod = u_lo * z[:, :half] + u_hi * z[:, half:]
        part = prod[:, 0:LANES]
        for cch in range(1, half // LANES):
            part = part + prod[:, cch * LANES:(cch + 1) * LANES]
        dots = jnp.sum(part, axis=1, keepdims=True)
        act = 0.5 * dots * (1.0 + lax.erf(dots * (2.0 ** -0.5)))
        g_col = pltpu.roll(gate_t, (tp - t) % tp, 1)[:, 0:1]
        wgt = g_col * act
        v_lo, v_hi = unpack(x[:, half:])
        mix = jnp.concatenate([jnp.sum(wgt * v_lo, axis=0, keepdims=True),
                               jnp.sum(wgt * v_hi, axis=0, keepdims=True)], axis=1)
        o_ref[pl.ds(t, 1), :] = x1_ref[pl.ds(t, 1), :] + mix

    @pl.when(step == 0)
    def _():
        idx_copy(0, 0).start()
        idx_copy(0, 0).wait()
        for t in range(ahead):
            issue(t, t, 0)

    @pl.when(has_next)
    def _():
        idx_copy(step + 1, 1 - cur).start()

    def group(g, last):
        for s in range(PEER_BUFS):
            t = g * PEER_BUFS + s
            wait(s)
            if not last or s == 0:
                issue(t + ahead, (s + ahead) % PEER_BUFS)
            compute(t, s)
            if last and s < ahead:
                @pl.when(has_next)
                def _():
                    if s == 0:
                        idx_copy(step + 1, 1 - cur).wait()
                    issue(s, s, (1 - cur) * n_idx)

    n_groups = tp // PEER_BUFS

    def body(g, carry):
        group(g, False)
        return carry

    lax.fori_loop(0, n_groups - 1, body, 0)
    group(n_groups - 1, True)


def _peer_mix(eidx, gate, zn, x1, uv):
    t, d = zn.shape
    tp = PEER_TILE
    nt = t // tp
    return pl.pallas_call(
        _peer_mix_kernel,
        grid=(nt,),
        in_specs=[pl.BlockSpec(memory_space=pl.ANY),
                  pl.BlockSpec(memory_space=pl.ANY),
                  pl.BlockSpec(memory_space=pl.ANY),
                  pl.BlockSpec((1, PEER_ROWS, tp), lambda i: (i, 0, 0)),
                  pl.BlockSpec((tp, d), lambda i: (i, 0)),
                  pl.BlockSpec((tp, d), lambda i: (i, 0))],
        out_specs=pl.BlockSpec((tp, d), lambda i: (i, 0)),
        out_shape=jax.ShapeDtypeStruct((t, d), F32),
        scratch_shapes=([pltpu.SMEM((2 * PEER_ROWS * tp,), I32)]
                        + [pltpu.VMEM((PEER_ROWS, d), I32) for _ in range(PEER_BUFS)]
                        + [pltpu.SemaphoreType.DMA((PEER_BUFS,)), pltpu.SemaphoreType.DMA((2,))]),
        compiler_params=_params(("arbitrary",), 32),
        name="peer_mix",
    )(eidx.reshape(nt, PEER_ROWS * tp), uv.reshape(-1), uv, gate, zn, x1)


def _pack_pairs(tab):
    bits = lax.bitcast_convert_type(tab.astype(BF16), jnp.uint16).astype(jnp.uint32)
    half = tab.shape[1] // 2
    return lax.bitcast_convert_type(bits[:, :half] | (bits[:, half:] << 16), I32)


def _prepare(w_in, gdn_a_log_fwd, gdn_a_log_bwd, gdn_dt_bias_fwd, gdn_dt_bias_bwd, mla_w_uq, mla_w_ukv,
             mla_qk_norm_q, mla_qk_norm_k, peer_u, peer_v, **others):
    o = 0
    parts = {}
    for name, width in (("qkvz", 4096), ("small", 32), ("cq", MLA_Q_LORA), ("ckv", MLA_KV_LORA),
                        ("kpe", MLA_ROPE), ("mq", MEM_HEADS * MEM_DH), ("gate", 3 * D_MODEL)):
        parts[name] = w_in[:, o:o + width]
        o += width
    wts = dict(others)
    wts["w_main"] = jnp.concatenate(
        [parts["qkvz"], parts["gate"], parts["mq"], parts["cq"], parts["ckv"]], axis=1).astype(BF16)
    wts["w_small"] = jnp.concatenate(
        [parts["small"], parts["kpe"], jnp.zeros((D_MODEL, LANES - 32 - MLA_ROPE), F32)],
        axis=1).astype(BF16)
    zeros16 = jnp.zeros((LANES - 16,), F32)
    wts["a_lane"] = jnp.concatenate(
        [jnp.exp(gdn_a_log_fwd.astype(F32)), jnp.exp(gdn_a_log_bwd.astype(F32)), zeros16]).reshape(1, LANES)
    wts["dtb_lane"] = jnp.concatenate(
        [gdn_dt_bias_fwd.astype(F32), gdn_dt_bias_bwd.astype(F32), zeros16]).reshape(1, LANES)
    pad_q = jnp.zeros((MLA_Q_LORA, MLA_HEADS, LANES - MLA_QK), F32)
    wts["w_uq"] = jnp.concatenate(
        [mla_w_uq.reshape(MLA_Q_LORA, MLA_HEADS, MLA_QK), pad_q], axis=2
    ).reshape(MLA_Q_LORA, MLA_HEADS * LANES).astype(BF16)
    ukv = mla_w_ukv.reshape(MLA_KV_LORA, MLA_HEADS, MLA_NOPE + MLA_V)
    pad_k = jnp.zeros((MLA_KV_LORA, MLA_HEADS, LANES - MLA_NOPE), F32)
    wts["w_uk"] = jnp.concatenate([ukv[:, :, :MLA_NOPE], pad_k], axis=2).reshape(
        MLA_KV_LORA, MLA_HEADS * LANES).astype(BF16)
    wts["w_uv"] = ukv[:, :, MLA_NOPE:].reshape(MLA_KV_LORA, MLA_HEADS * MLA_V).astype(BF16)
    pad_n = jnp.zeros((LANES - MLA_QK,), F32)
    wts["qk_norm_q"] = jnp.concatenate([mla_qk_norm_q, pad_n]).reshape(1, LANES)
    wts["qk_norm_k"] = jnp.concatenate([mla_qk_norm_k, pad_n]).reshape(1, LANES)
    for name in ("mla_q_norm", "mla_kv_norm", "mem_qk_norm_q", "mem_qk_norm_k", "g_ffn"):
        wts[name] = wts[name].reshape(1, -1)
    for name in ("w_branch_gdn", "w_branch_mla", "w_branch_mem", "w_out", "peer_w_q", "mem_w_kv",
                 "peer_keys_a", "peer_keys_b"):
        wts[name] = wts[name].astype(BF16)
    wts["peer_uv"] = jnp.concatenate([_pack_pairs(peer_u), _pack_pairs(peer_v)], axis=1)
    return wts


def _rope_tables(s):
    half = MLA_ROPE // 2
    inv = ROPE_THETA ** (-jnp.arange(half, dtype=F32) / half)
    ang = jnp.arange(s).astype(F32)[:, None] * inv[None, :]
    cos = jnp.cos(ang)
    sin = jnp.sin(ang)
    ones = jnp.ones((s, MLA_NOPE), F32)
    pad = jnp.zeros((s, LANES - MLA_QK), F32)
    zeros_n = jnp.zeros((s, MLA_NOPE), F32)
    zeros_h = jnp.zeros((s, half), F32)
    cos_t = jnp.concatenate([ones, cos, cos, pad], axis=1)
    sin_a = jnp.concatenate([zeros_n, zeros_h, sin, pad], axis=1)
    sin_b = jnp.concatenate([zeros_n, -sin, zeros_h, pad], axis=1)
    return cos_t, sin_a, sin_b


def _layer(x, mem, wts):
    b, s, d = x.shape
    t = b * s
    p_main, p_small = _norm_proj(x.reshape(t, d), wts["g_mix"], wts["w_main"], wts["w_small"], 5, BF16)
    p3 = p_main.reshape(b, s, N_MAIN)
    ps3 = p_small.reshape(b, s, LANES)
    qkv = _gdn_conv(p3, wts["gdn_conv_w"])
    o_fwd = _gdn_scan(qkv, ps3, wts["a_lane"], wts["dtb_lane"], False)
    o_gdn = _gdn_scan(qkv, ps3, wts["a_lane"], wts["dtb_lane"], True, o_fwd, p3, wts["gdn_out_norm"])
    q_m, k_m, v_m = _mla_prep(p3, ps3, _rope_tables(s), wts)
    o_mla = _attention(q_m, k_m, v_m)
    (mkv,) = _norm_proj(mem.reshape(b * MEM_TOKENS, d), wts["mem_norm"], wts["mem_w_kv"], None, 1, F32)
    mkv = mkv.reshape(b, MEM_TOKENS, 2 * MEM_HEADS * MEM_DH)
    x1, zn, qq = _merge(x, o_gdn, o_mla, p3, mkv, wts)
    eidx, gate = _peer_topk(qq.reshape(t, -1), wts["peer_keys_a"], wts["peer_keys_b"])
    y = _peer_mix(eidx, gate, zn.reshape(t, d), x1.reshape(t, d), wts["peer_uv"])
    return y.reshape(b, s, d)


def kernel(x_prompt, x_sample, mem_prompt, mem_sample, g_mix, w_in, gdn_conv_w, gdn_a_log_fwd,
           gdn_a_log_bwd, gdn_dt_bias_fwd, gdn_dt_bias_bwd, gdn_out_norm, mla_q_norm, mla_w_uq,
           mla_kv_norm, mla_w_ukv, mla_qk_norm_q, mla_qk_norm_k, mem_norm, mem_w_kv, mem_qk_norm_q,
           mem_qk_norm_k, w_branch_gdn, w_branch_mla, w_branch_mem, w_out, g_ffn, peer_w_q,
           peer_keys_a, peer_keys_b, peer_u, peer_v):
    wts = _prepare(
        w_in, gdn_a_log_fwd, gdn_a_log_bwd, gdn_dt_bias_fwd, gdn_dt_bias_bwd, mla_w_uq, mla_w_ukv,
        mla_qk_norm_q, mla_qk_norm_k, peer_u, peer_v,
        g_mix=g_mix, gdn_conv_w=gdn_conv_w, gdn_out_norm=gdn_out_norm, mla_q_norm=mla_q_norm,
        mla_kv_norm=mla_kv_norm, mem_norm=mem_norm, mem_w_kv=mem_w_kv, mem_qk_norm_q=mem_qk_norm_q,
        mem_qk_norm_k=mem_qk_norm_k, w_branch_gdn=w_branch_gdn, w_branch_mla=w_branch_mla,
        w_branch_mem=w_branch_mem, w_out=w_out, g_ffn=g_ffn, peer_w_q=peer_w_q,
        peer_keys_a=peer_keys_a, peer_keys_b=peer_keys_b)
    return (_layer(x_prompt, mem_prompt, wts), _layer(x_sample, mem_sample, wts))
```

```python
import functools

import jax
import jax.numpy as jnp
from jax import lax
from jax.experimental import pallas as pl
from jax.experimental.pallas import tpu as pltpu

F32 = jnp.float32
BF16 = jnp.bfloat16
I32 = jnp.int32

D_MODEL = 1024
GDN_HEADS = 8
GDN_DK = 128
GDN_DV = 128
GDN_CONV = 5
GDN_CHUNK = 64
MLA_HEADS = 8
MLA_NOPE = 64
MLA_ROPE = 32
MLA_QK = MLA_NOPE + MLA_ROPE
MLA_V = 64
MLA_Q_LORA = 384
MLA_KV_LORA = 256
ROPE_THETA = 10000.0
MEM_TOKENS = 256
MEM_HEADS = 4
MEM_DH = 128
PEER_HEADS = 8
PEER_DK = 256
PEER_KEYS = 128
PEER_TOPK = 16
NORM_EPS = 1e-6
LOG2E = 1.4426950408889634

LANES = 128
MIB = 1024 * 1024

C_Q, C_K, C_V, C_Z = 0, 1024, 2048, 3072
C_GATE = 4096
C_MQ = 7168
C_LAT = 7680
N_MAIN = 8320
L_AF, L_AB, L_BF, L_BB, L_KPE = 0, 8, 16, 24, 32

PEER_TILE = 128
PEER_ROWS = PEER_HEADS * PEER_TOPK
PEER_BUFS = 8


def _params(sem, vmem_mib):
    return pltpu.CompilerParams(dimension_semantics=sem, vmem_limit_bytes=vmem_mib * MIB)


def _mm(a, b):
    return jnp.dot(a.astype(BF16), b.astype(BF16), preferred_element_type=F32)


def _mm_nt(a, b):
    return lax.dot_general(a.astype(BF16), b.astype(BF16), (((1,), (1,)), ((), ())),
                           preferred_element_type=F32)


def _mm_tn(a, b):
    return lax.dot_general(a.astype(BF16), b.astype(BF16), (((0,), (0,)), ((), ())),
                           preferred_element_type=F32)


def _rms(x, n=None):
    n = x.shape[-1] if n is None else n
    return x * lax.rsqrt(jnp.sum(x * x, axis=-1, keepdims=True) * (1.0 / n) + NORM_EPS)


def _norm_proj_kernel(has_small, x_ref, g_ref, w_ref, *rest):
    if has_small:
        ws_ref, o_ref, os_ref, h_scr = rest
    else:
        o_ref, h_scr = rest

    @pl.when(pl.program_id(1) == 0)
    def _():
        h = _rms(x_ref[...]) * g_ref[...]
        h_scr[...] = h.astype(BF16)
        if has_small:
            os_ref[...] = jnp.dot(h_scr[...], ws_ref[...], preferred_element_type=F32)

    o_ref[...] = jnp.dot(h_scr[...], w_ref[...], preferred_element_type=F32).astype(o_ref.dtype)


def _norm_proj(x2d, g, w, w_small, n_col_blocks, out_dtype):
    t, d = x2d.shape
    n = w.shape[1]
    tm = min(1024, t)
    tn = n // n_col_blocks
    in_specs = [pl.BlockSpec((tm, d), lambda i, j: (i, 0)),
                pl.BlockSpec((1, d), lambda i, j: (0, 0)),
                pl.BlockSpec((d, tn), lambda i, j: (0, j))]
    out_shape = [jax.ShapeDtypeStruct((t, n), out_dtype)]
    out_specs = [pl.BlockSpec((tm, tn), lambda i, j: (i, j))]
    args = [x2d, g.reshape(1, d), w]
    if w_small is not None:
        in_specs.append(pl.BlockSpec((d, LANES), lambda i, j: (0, 0)))
        out_shape.append(jax.ShapeDtypeStruct((t, LANES), F32))
        out_specs.append(pl.BlockSpec((tm, LANES), lambda i, j: (i, 0)))
        args.append(w_small)
    return pl.pallas_call(
        functools.partial(_norm_proj_kernel, w_small is not None),
        grid=(t // tm, n_col_blocks),
        in_specs=in_specs, out_specs=out_specs, out_shape=out_shape,
        scratch_shapes=[pltpu.VMEM((tm, d), BF16)],
        compiler_params=_params(("parallel", "arbitrary"), 48),
        name="norm_proj",
    )(*args)


def _gdn_conv_kernel(ts, x_ref, w_ref, o_ref):
    s = x_ref.shape[1]
    n_chunks = s // ts
    halo = 16
    n_ext = ts + 2 * halo
    is_qk = pl.program_id(1) < 2 * GDN_HEADS
    w = w_ref[...]

    def body(i, carry):
        t0 = pl.multiple_of(i * ts, ts)
        main = x_ref[0, pl.ds(t0, ts), :].astype(F32)
        p0 = pl.multiple_of(jnp.maximum(t0 - halo, 0), halo)
        n0 = pl.multiple_of(jnp.minimum(t0 + ts, s - halo), halo)
        prev = jnp.where(i > 0, x_ref[0, pl.ds(p0, halo), :].astype(F32), 0.0)
        nxt = jnp.where(i < n_chunks - 1, x_ref[0, pl.ds(n0, halo), :].astype(F32), 0.0)
        ext = jnp.concatenate([prev, main, nxt], axis=0)
        acc = w[2:3, :] * ext
        for j in (0, 1, 3, 4):
            acc = acc + w[j:j + 1, :] * pltpu.roll(ext, (2 - j) % n_ext, 0)
        y = acc[halo:halo + ts]
        y = y * jax.nn.sigmoid(y)
        y_n = y * lax.rsqrt(jnp.sum(y * y, axis=-1, keepdims=True) + NORM_EPS)
        o_ref[0, 0, pl.ds(t0, ts), :] = jnp.where(is_qk, y_n, y).astype(o_ref.dtype)
        return carry

    lax.fori_loop(0, n_chunks, body, 0)


def _gdn_conv(p3, conv_w):
    b, s, _ = p3.shape
    ts = min(512, s)
    n_blk = 3 * GDN_HEADS
    return pl.pallas_call(
        functools.partial(_gdn_conv_kernel, ts),
        grid=(b, n_blk),
        in_specs=[pl.BlockSpec((1, s, LANES), lambda bi, c: (bi, 0, c)),
                  pl.BlockSpec((GDN_CONV, LANES), lambda bi, c: (0, c))],
        out_specs=pl.BlockSpec((1, 1, s, LANES), lambda bi, c: (bi, c, 0, 0)),
        out_shape=jax.ShapeDtypeStruct((b, n_blk, s, LANES), BF16),
        compiler_params=_params(("parallel", "parallel"), 40),
        name="gdn_conv",
    )(p3, conv_w)


def _gdn_scan_kernel(reverse, hb, ts, *refs):
    if reverse:
        q_ref, k_ref, v_ref, ps_ref, al_ref, dtb_ref, of_ref, z_ref, gn_ref, o_ref, s_scr = refs
    else:
        q_ref, k_ref, v_ref, ps_ref, al_ref, dtb_ref, o_ref, s_scr = refs
    c = GDN_CHUNK
    n_chunks = ts // c

    @pl.when(pl.program_id(2) == 0)
    def _():
        s_scr[...] = jnp.zeros_like(s_scr)

    ps = ps_ref[0]
    g = -al_ref[...] * jax.nn.softplus(ps + dtb_ref[...])
    beta = jax.nn.sigmoid(ps)
    rowid = lax.broadcasted_iota(I32, (ts, LANES), 0) & (c - 1)
    gc = g
    for kk in (1, 2, 4, 8, 16, 32):
        if reverse:
            gc = gc + jnp.where(rowid < c - kk, pltpu.roll(gc, ts - kk, 0), 0.0)
        else:
            gc = gc + jnp.where(rowid >= kk, pltpu.roll(gc, kk, 0), 0.0)
    shift = (LANES - pl.program_id(1) * hb) % LANES
    gc = pltpu.roll(gc, shift, 1)
    beta = pltpu.roll(beta, shift, 1)
    eg = jnp.exp(gc)
    gc_t = [gc[p * LANES:(p + 1) * LANES, :].T for p in range(ts // LANES)]

    r = lax.broadcasted_iota(I32, (c, c), 0)
    cc = lax.broadcasted_iota(I32, (c, c), 1)
    if reverse:
        r, cc = cc, r
    incl = r >= cc
    strict = r > cc
    eye = jnp.where(r == cc, 1.0, 0.0)
    level_f = [jnp.where((((r >> s) & 1) == 1) & ((cc >> s) == (r >> s) - 1), 1.0, 0.0)
               for s in range(6)]

    lane_g = L_AB if reverse else L_AF
    lane_b = L_BB if reverse else L_BF
    scale = GDN_DK ** -0.5
    order = list(range(n_chunks - 1, -1, -1)) if reverse else list(range(n_chunks))
    units = [(ci, hh) for ci in order for hh in range(hb)]

    kb16, k16, rhs, dec, kd16, a2q, gl = {}, {}, {}, {}, {}, {}, {}
    for un in units:
        ci, hh = un
        r0 = ci * c
        q = q_ref[0, hh, r0:r0 + c, :].astype(F32) * scale
        k = k_ref[0, hh, r0:r0 + c, :].astype(F32)
        v = v_ref[0, hh, r0:r0 + c, :].astype(F32)
        la = lane_g + hh
        lb = lane_b + hh
        gc_c = gc[r0:r0 + c, la:la + 1]
        eg_c = eg[r0:r0 + c, la:la + 1]
        b_c = beta[r0:r0 + c, lb:lb + 1]
        gc_r = gc_t[r0 // LANES][la:la + 1, r0 % LANES:r0 % LANES + c]
        r_last = r0 if reverse else r0 + c - 1
        gc_l = gc[r_last:r_last + 1, la:la + 1]
        dec[un] = jnp.exp(jnp.where(incl, gc_c - gc_r, -1e30))
        kb = k * b_c
        kb16[un] = jnp.concatenate([kb, q], axis=0).astype(BF16)
        k16[un] = k.astype(BF16)
        rhs[un] = jnp.concatenate([v * b_c, kb * eg_c], axis=1).astype(BF16)
        a2q[un] = (q * eg_c).astype(BF16)
        kd16[un] = (k * jnp.exp(gc_l - gc_c)).astype(BF16)
        gl[un] = jnp.exp(gc_l)
    kq = {un: lax.dot_general(kb16[un], k16[un], (((1,), (1,)), ((), ())),
                              preferred_element_type=F32) for un in units}
    lm = {un: jnp.where(strict, kq[un][:c] * dec[un], 0.0) for un in units}
    intra = {un: jnp.where(incl, kq[un][c:] * dec[un], 0.0).astype(BF16) for un in units}
    dm = {un: eye - lm[un] * level_f[0] for un in units}
    for s in range(1, 6):
        cd = {un: _mm(lm[un] * level_f[s], dm[un]) for un in units}
        dm = {un: dm[un] - _mm(dm[un], cd[un]) for un in units}
    sol = {un: _mm(dm[un], rhs[un]) for un in units}

    st = [s_scr[hh] for hh in range(hb)]
    for ci in order:
        r0 = ci * c
        heads = [(ci, hh) for hh in range(hb)]
        a2 = [jnp.concatenate([sol[un][:, GDN_DV:].astype(BF16), a2q[un]], axis=0) for un in heads]
        ws_qs = [jnp.dot(a2[hh], st[hh].astype(BF16), preferred_element_type=F32) for hh in range(hb)]
        v_new = [sol[un][:, :GDN_DV] - ws_qs[un[1]][:c] for un in heads]
        v16 = [vn.astype(BF16) for vn in v_new]
        o = [ws_qs[hh][c:] + jnp.dot(intra[(ci, hh)], v16[hh], preferred_element_type=F32)
             for hh in range(hb)]
        st = [st[hh] * gl[(ci, hh)]
              + lax.dot_general(kd16[(ci, hh)], v16[hh], (((0,), (0,)), ((), ())),
                                preferred_element_type=F32) for hh in range(hb)]
        for hh in range(hb):
            cols = slice(hh * GDN_DV, (hh + 1) * GDN_DV)
            if reverse:
                tot = o[hh] + of_ref[0, r0:r0 + c, cols]
                zz = z_ref[0, r0:r0 + c, cols].astype(F32)
                y = _rms(tot) * gn_ref[...] * (zz * jax.nn.sigmoid(zz))
                o_ref[0, r0:r0 + c, cols] = y.astype(o_ref.dtype)
            else:
                o_ref[0, r0:r0 + c, cols] = o[hh]
    for hh in range(hb):
        s_scr[hh] = st[hh]


def _gdn_scan(qkv, ps3, a_lane, dtb_lane, reverse, o_fwd=None, p3=None, out_norm=None):
    b, _, s, _ = qkv.shape
    hb = 8
    ts = 256
    nt = s // ts
    n_hblk = GDN_HEADS // hb
    tile = (lambda i: nt - 1 - i) if reverse else (lambda i: i)
    in_specs = [
        pl.BlockSpec((1, hb, ts, LANES), lambda bi, h, i: (bi, h, tile(i), 0)),
        pl.BlockSpec((1, hb, ts, LANES), lambda bi, h, i: (bi, n_hblk + h, tile(i), 0)),
        pl.BlockSpec((1, hb, ts, LANES), lambda bi, h, i: (bi, 2 * n_hblk + h, tile(i), 0)),
        pl.BlockSpec((1, ts, LANES), lambda bi, h, i: (bi, tile(i), 0)),
        pl.BlockSpec((1, LANES), lambda bi, h, i: (0, 0)),
        pl.BlockSpec((1, LANES), lambda bi, h, i: (0, 0)),
    ]
    args = [qkv, qkv, qkv, ps3, a_lane, dtb_lane]
    wide = hb * GDN_DV
    if reverse:
        in_specs += [
            pl.BlockSpec((1, ts, wide), lambda bi, h, i: (bi, tile(i), h)),
            pl.BlockSpec((1, ts, wide), lambda bi, h, i: (bi, tile(i), C_Z // wide + h)),
            pl.BlockSpec((1, GDN_DV), lambda bi, h, i: (0, 0)),
        ]
        args += [o_fwd, p3, out_norm.reshape(1, GDN_DV)]
    out_dtype = BF16 if reverse else F32
    return pl.pallas_call(
        functools.partial(_gdn_scan_kernel, reverse, hb, ts),
        grid=(b, n_hblk, nt),
        in_specs=in_specs,
        out_specs=pl.BlockSpec((1, ts, wide), lambda bi, h, i: (bi, tile(i), h)),
        out_shape=jax.ShapeDtypeStruct((b, s, GDN_HEADS * GDN_DV), out_dtype),
        scratch_shapes=[pltpu.VMEM((hb, GDN_DK, GDN_DV), F32)],
        compiler_params=_params(("parallel", "parallel", "arbitrary"), 32),
        name="gdn_scan_bwd" if reverse else "gdn_scan_fwd",
    )(*args)


def _mla_prep_kernel(lat_ref, ps_ref, cos_ref, sa_ref, sb_ref, gq_ref, gkv_ref, wuq_ref, wuk_ref,
                     wuv_ref, nq_ref, nk_ref, q_ref, k_ref, v_ref):
    lat = lat_ref[0].astype(F32)
    cq = _rms(lat[:, :MLA_Q_LORA]) * gq_ref[...]
    ckv = _rms(lat[:, MLA_Q_LORA:]) * gkv_ref[...]
    q_all = _mm(cq, wuq_ref[...])
    k_all = _mm(ckv, wuk_ref[...])
    v_ref[0] = _mm(ckv, wuv_ref[...]).astype(v_ref.dtype)
    ps = ps_ref[0]
    lane = lax.broadcasted_iota(I32, ps.shape, 1)
    in_rope = (lane >= MLA_NOPE) & (lane < MLA_QK)
    kpe = jnp.where(in_rope, pltpu.roll(ps, MLA_NOPE - L_KPE, 1), 0.0)
    cos = cos_ref[...]
    sin_a = sa_ref[...]
    sin_b = sb_ref[...]
    half = MLA_ROPE // 2

    def norm_rope(x, gain):
        y = _rms(x, MLA_QK) * gain
        return y * cos + pltpu.roll(y, half, 1) * sin_a + pltpu.roll(y, LANES - half, 1) * sin_b

    for h in range(MLA_HEADS):
        cols = slice(h * LANES, (h + 1) * LANES)
        q_ref[0, h] = (norm_rope(q_all[:, cols], nq_ref[...]) * (MLA_QK ** -0.5 * LOG2E)).astype(q_ref.dtype)
        k_ref[0, h] = norm_rope(k_all[:, cols] + kpe, nk_ref[...]).astype(k_ref.dtype)


def _mla_prep(p3, ps3, rope_tabs, wts):
    b, s, _ = p3.shape
    tm = min(512, s)
    lat_w = MLA_Q_LORA + MLA_KV_LORA
    full = lambda shape: pl.BlockSpec(shape, lambda bi, i: tuple(0 for _ in shape))
    tab = pl.BlockSpec((tm, LANES), lambda bi, i: (i, 0))
    hw = MLA_HEADS * LANES
    return pl.pallas_call(
        _mla_prep_kernel,
        grid=(b, s // tm),
        in_specs=[pl.BlockSpec((1, tm, lat_w), lambda bi, i: (bi, i, C_LAT // lat_w)),
                  pl.BlockSpec((1, tm, LANES), lambda bi, i: (bi, i, 0)),
                  tab, tab, tab,
                  full((1, MLA_Q_LORA)), full((1, MLA_KV_LORA)),
                  full((MLA_Q_LORA, hw)), full((MLA_KV_LORA, hw)),
                  full((MLA_KV_LORA, MLA_HEADS * MLA_V)),
                  full((1, LANES)), full((1, LANES))],
        out_specs=[pl.BlockSpec((1, MLA_HEADS, tm, LANES), lambda bi, i: (bi, 0, i, 0)),
                   pl.BlockSpec((1, MLA_HEADS, tm, LANES), lambda bi, i: (bi, 0, i, 0)),
                   pl.BlockSpec((1, tm, MLA_HEADS * MLA_V), lambda bi, i: (bi, i, 0))],
        out_shape=[jax.ShapeDtypeStruct((b, MLA_HEADS, s, LANES), BF16),
                   jax.ShapeDtypeStruct((b, MLA_HEADS, s, LANES), BF16),
                   jax.ShapeDtypeStruct((b, s, MLA_HEADS * MLA_V), BF16)],
        compiler_params=_params(("parallel", "parallel"), 40),
        name="mla_prep",
    )(p3, ps3, *rope_tabs, wts["mla_q_norm"], wts["mla_kv_norm"], wts["w_uq"], wts["w_uk"],
      wts["w_uv"], wts["qk_norm_q"], wts["qk_norm_k"])


def _attn_kernel(tk_sub, q_ref, k_ref, v_ref, o_ref, m_scr, l_scr, acc_scr):
    j = pl.program_id(3)

    @pl.when(j == 0)
    def _():
        m_scr[...] = jnp.full_like(m_scr, -jnp.inf)
        l_scr[...] = jnp.zeros_like(l_scr)
        acc_scr[...] = jnp.zeros_like(acc_scr)

    n_sub = k_ref.shape[2] // tk_sub
    n_lt = tk_sub // LANES
    m = [m_scr[hh] for hh in range(2)]
    l = [l_scr[hh] for hh in range(2)]
    acc = [acc_scr[hh] for hh in range(2)]
    for sb in range(n_sub):
        rows = slice(sb * tk_sub, (sb + 1) * tk_sub)
        v = v_ref[0, rows, :]
        for hh in range(2):
            s = lax.dot_general(q_ref[0, hh], k_ref[0, hh, rows, :], (((1,), (1,)), ((), ())),
                                preferred_element_type=F32)
            tiles = [s[:, t * LANES:(t + 1) * LANES] for t in range(n_lt)]
            mx = tiles[0]
            for t in tiles[1:]:
                mx = jnp.maximum(mx, t)
            m_new = jnp.maximum(m[hh], jnp.max(mx, axis=-1, keepdims=True))
            alpha = jnp.exp2(m[hh] - m_new)
            ps = [jnp.exp2(t - m_new) for t in tiles]
            psum = ps[0]
            for t in ps[1:]:
                psum = psum + t
            l[hh] = alpha * l[hh] + psum
            p = jnp.concatenate(ps, axis=1).astype(BF16)
            acc[hh] = alpha * acc[hh] + jnp.dot(p, v, preferred_element_type=F32)
            m[hh] = m_new
    for hh in range(2):
        m_scr[hh] = m[hh]
        l_scr[hh] = l[hh]
        acc_scr[hh] = acc[hh]

    @pl.when(j == pl.num_programs(3) - 1)
    def _():
        lane = lax.broadcasted_iota(I32, o_ref.shape[1:], 1)
        o0 = acc[0] / jnp.sum(l[0], axis=-1, keepdims=True)
        o1 = acc[1] / jnp.sum(l[1], axis=-1, keepdims=True)
        o_ref[0] = jnp.where(lane < MLA_V, o0, o1).astype(o_ref.dtype)


def _attention(q, k, v):
    b, h, s, _ = q.shape
    tq = min(512, s)
    tk = min(4096, s)
    tk_sub = min(256, tk)
    return pl.pallas_call(
        functools.partial(_attn_kernel, tk_sub),
        grid=(b, h // 2, s // tq, s // tk),
        in_specs=[pl.BlockSpec((1, 2, tq, LANES), lambda bi, p, i, j: (bi, p, i, 0)),
                  pl.BlockSpec((1, 2, tk, LANES), lambda bi, p, i, j: (bi, p, j, 0)),
                  pl.BlockSpec((1, tk, LANES), lambda bi, p, i, j: (bi, j, p))],
        out_specs=pl.BlockSpec((1, tq, LANES), lambda bi, p, i, j: (bi, i, p)),
        out_shape=jax.ShapeDtypeStruct((b, s, h * MLA_V), BF16),
        scratch_shapes=[pltpu.VMEM((2, tq, LANES), F32), pltpu.VMEM((2, tq, LANES), F32),
                        pltpu.VMEM((2, tq, LANES), F32)],
        compiler_params=_params(("parallel", "parallel", "parallel", "arbitrary"), 40),
        name="mla_attention",
    )(q, k, v)


def _merge_kernel(x_ref, og_ref, om_ref, g0_ref, g1_ref, g2_ref, mq_ref, mkv_ref, wbg_ref, wbm_ref,
                  wbe_ref, wo_ref, wq_ref, nq_ref, nk_ref, gf_ref, x1_ref, zn_ref, qq_ref):
    mq = mq_ref[0].astype(F32)
    mkv = mkv_ref[0]
    kv_w = MEM_HEADS * MEM_DH
    outs = []
    for h in range(MEM_HEADS):
        cols = slice(h * MEM_DH, (h + 1) * MEM_DH)
        qh = _rms(mq[:, cols]) * nq_ref[...]
        kh = _rms(mkv[:, cols]) * nk_ref[...]
        vh = mkv[:, kv_w + h * MEM_DH:kv_w + (h + 1) * MEM_DH]
        s = _mm_nt(qh, kh) * (MEM_DH ** -0.5)
        p = jnp.exp(s - jnp.max(s, axis=-1, keepdims=True))
        p = p / jnp.sum(p, axis=-1, keepdims=True)
        outs.append(_mm(p, vh))
    y_mem = _mm(jnp.concatenate(outs, axis=1), wbe_ref[...])
    y_gdn = jnp.dot(og_ref[0], wbg_ref[...], preferred_element_type=F32)
    y_mla = jnp.dot(om_ref[0], wbm_ref[...], preferred_element_type=F32)
    merged = (jax.nn.sigmoid(g0_ref[0].astype(F32)) * y_gdn
              + jax.nn.sigmoid(g1_ref[0].astype(F32)) * y_mla
              + jax.nn.sigmoid(g2_ref[0].astype(F32)) * y_mem)
    x1 = x_ref[0] + _mm(merged, wo_ref[...])
    x1_ref[0] = x1
    zn = _rms(x1) * gf_ref[...]
    zn_ref[0] = zn
    qq_ref[0] = _mm(zn, wq_ref[...])


def _merge(x, o_gdn, o_mla, p3, mkv, wts):
    b, s, d = x.shape
    tm = min(256, s)
    full = lambda shape: pl.BlockSpec(shape, lambda bi, i: tuple(0 for _ in shape))
    tok = lambda w, cb: pl.BlockSpec((1, tm, w), lambda bi, i: (bi, i, cb))
    nq = PEER_HEADS * PEER_DK
    kv_w = MEM_HEADS * MEM_DH
    return pl.pallas_call(
        _merge_kernel,
        grid=(b, s // tm),
        in_specs=[tok(d, 0), tok(d, 0), tok(MLA_HEADS * MLA_V, 0),
                  tok(d, C_GATE // d), tok(d, C_GATE // d + 1), tok(d, C_GATE // d + 2),
                  tok(kv_w, C_MQ // kv_w),
                  pl.BlockSpec((1, MEM_TOKENS, 2 * kv_w), lambda bi, i: (bi, 0, 0)),
                  full((d, d)), full((MLA_HEADS * MLA_V, d)), full((kv_w, d)), full((d, d)),
                  full((d, nq)), full((1, MEM_DH)), full((1, MEM_DH)), full((1, d))],
        out_specs=[tok(d, 0), tok(d, 0), tok(nq, 0)],
        out_shape=[jax.ShapeDtypeStruct((b, s, d), F32), jax.ShapeDtypeStruct((b, s, d), F32),
                   jax.ShapeDtypeStruct((b, s, nq), F32)],
        compiler_params=_params(("parallel", "parallel"), 52),
        name="merge",
    )(x, o_gdn, o_mla, p3, p3, p3, p3, mkv, wts["w_branch_gdn"], wts["w_branch_mla"],
      wts["w_branch_mem"], wts["w_out"], wts["peer_w_q"], wts["mem_qk_norm_q"],
      wts["mem_qk_norm_k"], wts["g_ffn"])


def _top16(cur, order_key=None, payload=None):
    n, t = cur.shape
    if order_key is None:
        order_key = lax.broadcasted_iota(I32, (n, t), 0)
    order_key = order_key.astype(F32)
    payload = None if payload is None else payload.astype(F32)
    slot = lax.broadcasted_iota(I32, (PEER_TOPK, t), 0)
    vals = jnp.zeros((PEER_TOPK, t), F32)
    picks = jnp.zeros((PEER_TOPK, t), F32)
    for r in range(PEER_TOPK):
        m = jnp.max(cur, axis=0, keepdims=True)
        am = jnp.min(jnp.where(cur == m, order_key, 2.0 ** 30), axis=0, keepdims=True)
        hit = order_key == am
        pick = am if payload is None else jnp.max(jnp.where(hit, payload, -1.0), axis=0, keepdims=True)
        vals = jnp.where(slot == r, m, vals)
        picks = jnp.where(slot == r, pick, picks)
        cur = jnp.where(hit, -jnp.inf, cur)
    return vals, picks.astype(I32)


def _staircase(x1, x2):
    lo1, hi1, lo2, hi2 = x1[0:8], x1[8:16], x2[0:8], x2[8:16]
    return jnp.concatenate(
        [x1[0:1] + lo2, x1[0:1] + hi2, x1[1:2] + lo2, x1[2:3] + lo2, x1[3:4] + lo2, x1[4:5] + lo2,
         hi1 + x2[0:1], lo1 + x2[0:1], lo1 + x2[1:2]], axis=0)


def _staircase_layout(t):
    k = lax.broadcasted_iota(I32, (8, t), 0)
    yes = k >= 0
    pos = jnp.concatenate([k, 8 + k, 16 + k, 32 + k, 48 + k, 64 + k, (8 + k) * 16, k * 16, k * 16 + 1],
                          axis=0)
    valid = jnp.concatenate([yes, yes, yes, k < 5, k < 4, k < 3, yes, k >= 5, k >= 5], axis=0)
    return pos, valid


def _peer_topk_kernel(q_ref, ka_ref, kb_ref, eidx_ref, gate_ref):
    half = PEER_DK // 2
    pos, valid = _staircase_layout(q_ref.shape[0])
    for h in range(PEER_HEADS):
        q1 = q_ref[:, h * PEER_DK:h * PEER_DK + half]
        q2 = q_ref[:, h * PEER_DK + half:(h + 1) * PEER_DK]
        s1 = _mm_nt(ka_ref[h], q1)
        s2 = _mm_nt(kb_ref[h], q2)
        v1, i1 = _top16(s1)
        v2, i2 = _top16(s2)
        cand = jnp.where(valid, _staircase(v1, v2), -jnp.inf)
        cidx = _staircase(i1 * PEER_KEYS, i2)
        top_s, eidx = _top16(cand, pos, cidx)
        p = jnp.exp(top_s - top_s[0:1, :])
        rows = slice(h * PEER_TOPK, (h + 1) * PEER_TOPK)
        gate_ref[0, rows, :] = p / jnp.sum(p, axis=0, keepdims=True)
        eidx_ref[0, rows, :] = eidx * D_MODEL


def _peer_topk(qq, keys_a, keys_b):
    t, nq = qq.shape
    tp = PEER_TILE
    nt = t // tp
    kspec = pl.BlockSpec((PEER_HEADS, PEER_KEYS, PEER_DK // 2), lambda i: (0, 0, 0))
    return pl.pallas_call(
        _peer_topk_kernel,
        grid=(nt,),
        in_specs=[pl.BlockSpec((tp, nq), lambda i: (i, 0)), kspec, kspec],
        out_specs=[pl.BlockSpec((1, PEER_ROWS, tp), lambda i: (i, 0, 0)),
                   pl.BlockSpec((1, PEER_ROWS, tp), lambda i: (i, 0, 0))],
        out_shape=[jax.ShapeDtypeStruct((nt, PEER_ROWS, tp), I32),
                   jax.ShapeDtypeStruct((nt, PEER_ROWS, tp), F32)],
        compiler_params=_params(("parallel",), 32),
        name="peer_topk",
    )(qq, keys_a, keys_b)


def _peer_mix_kernel(eidx_hbm, uv_flat, uv_rows, gate_ref, zn_ref, x1_ref, o_ref, idx_smem, *rest):
    bufs = rest[:PEER_BUFS]
    sem, isem = rest[PEER_BUFS:]
    tp = zn_ref.shape[0]
    half = D_MODEL // 2
    n_idx = PEER_ROWS * tp
    step = pl.program_id(0)
    cur = step % 2

    def idx_copy(s, b):
        dst = idx_smem.at[pl.ds(pl.multiple_of(b * n_idx, n_idx), n_idx)]
        return pltpu.make_async_copy(eidx_hbm.at[s], dst, isem.at[b])

    has_next = step + 1 < pl.num_programs(0)
    ahead = PEER_BUFS - 1
    idx_base = cur * n_idx
    gate_t = gate_ref[0]

    def issue(t, slot, base=idx_base):
        off = base + t
        for j in range(PEER_ROWS):
            row_off = idx_smem[off + j * tp]
            src = uv_flat.at[pl.ds(pl.multiple_of(row_off, D_MODEL), D_MODEL)]
            pltpu.make_async_copy(src, bufs[slot].at[j], sem.at[slot]).start(priority=j % 2)

    def wait(slot):
        pltpu.make_async_copy(uv_rows.at[pl.ds(0, PEER_ROWS)], bufs[slot], sem.at[slot]).wait()

    def unpack(words):
        lo = lax.bitcast_convert_type(words << 16, F32)
        hi = lax.bitcast_convert_type(words & jnp.int32(-65536), F32)
        return lo, hi

    def compute(t, slot):
        x = bufs[slot][...]
        z = zn_ref[pl.ds(t, 1), :]
        u_lo, u_hi = unpack(x[:, :half])
        prod = u_lo * z[:, :half] + u_hi * z[:, half:]
        part = prod[:, 0:LANES]
        for cch in range(1, half // LANES):
            part = part + prod[:, cch * LANES:(cch + 1) * LANES]
        dots = jnp.sum(part, axis=1, keepdims=True)
        act = 0.5 * dots * (1.0 + lax.erf(dots * (2.0 ** -0.5)))
        g_col = pltpu.roll(gate_t, (tp - t) % tp, 1)[:, 0:1]
        wgt = g_col * act
        v_lo, v_hi = unpack(x[:, half:])
        mix = jnp.concatenate([jnp.sum(wgt * v_lo, axis=0, keepdims=True),
                               jnp.sum(wgt * v_hi, axis=0, keepdims=True)], axis=1)
        o_ref[pl.ds(t, 1), :] = x1_ref[pl.ds(t, 1), :] + mix

    @pl.when(step == 0)
    def _():
        idx_copy(0, 0).start()
        idx_copy(0, 0).wait()
        for t in range(ahead):
            issue(t, t, 0)

    @pl.when(has_next)
    def _():
        idx_copy(step + 1, 1 - cur).start()

    def group(g, last):
        for s in range(PEER_BUFS):
            t = g * PEER_BUFS + s
            wait(s)
            if not last or s == 0:
                issue(t + ahead, (s + ahead) % PEER_BUFS)
            compute(t, s)
            if last and s < ahead:
                @pl.when(has_next)
                def _():
                    if s == 0:
                        idx_copy(step + 1, 1 - cur).wait()
                    issue(s, s, (1 - cur) * n_idx)

    n_groups = tp // PEER_BUFS

    def body(g, carry):
        group(g, False)
        return carry

    lax.fori_loop(0, n_groups - 1, body, 0)
    group(n_groups - 1, True)


def _peer_mix(eidx, gate, zn, x1, uv):
    t, d = zn.shape
    tp = PEER_TILE
    nt = t // tp
    return pl.pallas_call(
        _peer_mix_kernel,
        grid=(nt,),
        in_specs=[pl.BlockSpec(memory_space=pl.ANY),
                  pl.BlockSpec(memory_space=pl.ANY),
                  pl.BlockSpec(memory_space=pl.ANY),
                  pl.BlockSpec((1, PEER_ROWS, tp), lambda i: (i, 0, 0)),
                  pl.BlockSpec((tp, d), lambda i: (i, 0)),
                  pl.BlockSpec((tp, d), lambda i: (i, 0))],
        out_specs=pl.BlockSpec((tp, d), lambda i: (i, 0)),
        out_shape=jax.ShapeDtypeStruct((t, d), F32),
        scratch_shapes=([pltpu.SMEM((2 * PEER_ROWS * tp,), I32)]
                        + [pltpu.VMEM((PEER_ROWS, d), I32) for _ in range(PEER_BUFS)]
                        + [pltpu.SemaphoreType.DMA((PEER_BUFS,)), pltpu.SemaphoreType.DMA((2,))]),
        compiler_params=_params(("arbitrary",), 32),
        name="peer_mix",
    )(eidx.reshape(nt, PEER_ROWS * tp), uv.reshape(-1), uv, gate, zn, x1)


def _pack_pairs(tab):
    bits = lax.bitcast_convert_type(tab.astype(BF16), jnp.uint16).astype(jnp.uint32)
    half = tab.shape[1] // 2
    return lax.bitcast_convert_type(bits[:, :half] | (bits[:, half:] << 16), I32)


def _prepare(w_in, gdn_a_log_fwd, gdn_a_log_bwd, gdn_dt_bias_fwd, gdn_dt_bias_bwd, mla_w_uq, mla_w_ukv,
             mla_qk_norm_q, mla_qk_norm_k, peer_u, peer_v, **others):
    o = 0
    parts = {}
    for name, width in (("qkvz", 4096), ("small", 32), ("cq", MLA_Q_LORA), ("ckv", MLA_KV_LORA),
                        ("kpe", MLA_ROPE), ("mq", MEM_HEADS * MEM_DH), ("gate", 3 * D_MODEL)):
        parts[name] = w_in[:, o:o + width]
        o += width
    wts = dict(others)
    wts["w_main"] = jnp.concatenate(
        [parts["qkvz"], parts["gate"], parts["mq"], parts["cq"], parts["ckv"]], axis=1).astype(BF16)
    wts["w_small"] = jnp.concatenate(
        [parts["small"], parts["kpe"], jnp.zeros((D_MODEL, LANES - 32 - MLA_ROPE), F32)],
        axis=1).astype(BF16)
    zeros16 = jnp.zeros((LANES - 16,), F32)
    wts["a_lane"] = jnp.concatenate(
        [jnp.exp(gdn_a_log_fwd.astype(F32)), jnp.exp(gdn_a_log_bwd.astype(F32)), zeros16]).reshape(1, LANES)
    wts["dtb_lane"] = jnp.concatenate(
        [gdn_dt_bias_fwd.astype(F32), gdn_dt_bias_bwd.astype(F32), zeros16]).reshape(1, LANES)
    pad_q = jnp.zeros((MLA_Q_LORA, MLA_HEADS, LANES - MLA_QK), F32)
    wts["w_uq"] = jnp.concatenate(
        [mla_w_uq.reshape(MLA_Q_LORA, MLA_HEADS, MLA_QK), pad_q], axis=2
    ).reshape(MLA_Q_LORA, MLA_HEADS * LANES).astype(BF16)
    ukv = mla_w_ukv.reshape(MLA_KV_LORA, MLA_HEADS, MLA_NOPE + MLA_V)
    pad_k = jnp.zeros((MLA_KV_LORA, MLA_HEADS, LANES - MLA_NOPE), F32)
    wts["w_uk"] = jnp.concatenate([ukv[:, :, :MLA_NOPE], pad_k], axis=2).reshape(
        MLA_KV_LORA, MLA_HEADS * LANES).astype(BF16)
    wts["w_uv"] = ukv[:, :, MLA_NOPE:].reshape(MLA_KV_LORA, MLA_HEADS * MLA_V).astype(BF16)
    pad_n = jnp.zeros((LANES - MLA_QK,), F32)
    wts["qk_norm_q"] = jnp.concatenate([mla_qk_norm_q, pad_n]).reshape(1, LANES)
    wts["qk_norm_k"] = jnp.concatenate([mla_qk_norm_k, pad_n]).reshape(1, LANES)
    for name in ("mla_q_norm", "mla_kv_norm", "mem_qk_norm_q", "mem_qk_norm_k", "g_ffn"):
        wts[name] = wts[name].reshape(1, -1)
    for name in ("w_branch_gdn", "w_branch_mla", "w_branch_mem", "w_out", "peer_w_q", "mem_w_kv",
                 "peer_keys_a", "peer_keys_b"):
        wts[name] = wts[name].astype(BF16)
    wts["peer_uv"] = jnp.concatenate([_pack_pairs(peer_u), _pack_pairs(peer_v)], axis=1)
    return wts


def _rope_tables(s):
    half = MLA_ROPE // 2
    inv = ROPE_THETA ** (-jnp.arange(half, dtype=F32) / half)
    ang = jnp.arange(s).astype(F32)[:, None] * inv[None, :]
    cos = jnp.cos(ang)
    sin = jnp.sin(ang)
    ones = jnp.ones((s, MLA_NOPE), F32)
    pad = jnp.zeros((s, LANES - MLA_QK), F32)
    zeros_n = jnp.zeros((s, MLA_NOPE), F32)
    zeros_h = jnp.zeros((s, half), F32)
    cos_t = jnp.concatenate([ones, cos, cos, pad], axis=1)
    sin_a = jnp.concatenate([zeros_n, zeros_h, sin, pad], axis=1)
    sin_b = jnp.concatenate([zeros_n, -sin, zeros_h, pad], axis=1)
    return cos_t, sin_a, sin_b


def _layer(x, mem, wts):
    b, s, d = x.shape
    t = b * s
    p_main, p_small = _norm_proj(x.reshape(t, d), wts["g_mix"], wts["w_main"], wts["w_small"], 5, BF16)
    p3 = p_main.reshape(b, s, N_MAIN)
    ps3 = p_small.reshape(b, s, LANES)
    qkv = _gdn_conv(p3, wts["gdn_conv_w"])
    o_fwd = _gdn_scan(qkv, ps3, wts["a_lane"], wts["dtb_lane"], False)
    o_gdn = _gdn_scan(qkv, ps3, wts["a_lane"], wts["dtb_lane"], True, o_fwd, p3, wts["gdn_out_norm"])
    q_m, k_m, v_m = _mla_prep(p3, ps3, _rope_tables(s), wts)
    o_mla = _attention(q_m, k_m, v_m)
    (mkv,) = _norm_proj(mem.reshape(b * MEM_TOKENS, d), wts["mem_norm"], wts["mem_w_kv"], None, 1, F32)
    mkv = mkv.reshape(b, MEM_TOKENS, 2 * MEM_HEADS * MEM_DH)
    x1, zn, qq = _merge(x, o_gdn, o_mla, p3, mkv, wts)
    eidx, gate = _peer_topk(qq.reshape(t, -1), wts["peer_keys_a"], wts["peer_keys_b"])
    y = _peer_mix(eidx, gate, zn.reshape(t, d), x1.reshape(t, d), wts["peer_uv"])
    return y.reshape(b, s, d)


def kernel(x_prompt, x_sample, mem_prompt, mem_sample, g_mix, w_in, gdn_conv_w, gdn_a_log_fwd,
           gdn_a_log_bwd, gdn_dt_bias_fwd, gdn_dt_bias_bwd, gdn_out_norm, mla_q_norm, mla_w_uq,
           mla_kv_norm, mla_w_ukv, mla_qk_norm_q, mla_qk_norm_k, mem_norm, mem_w_kv, mem_qk_norm_q,
           mem_qk_norm_k, w_branch_gdn, w_branch_mla, w_branch_mem, w_out, g_ffn, peer_w_q,
           peer_keys_a, peer_keys_b, peer_u, peer_v):
    wts = _prepare(
        w_in, gdn_a_log_fwd, gdn_a_log_bwd, gdn_dt_bias_fwd, gdn_dt_bias_bwd, mla_w_uq, mla_w_ukv,
        mla_qk_norm_q, mla_qk_norm_k, peer_u, peer_v,
        g_mix=g_mix, gdn_conv_w=gdn_conv_w, gdn_out_norm=gdn_out_norm, mla_q_norm=mla_q_norm,
        mla_kv_norm=mla_kv_norm, mem_norm=mem_norm, mem_w_kv=mem_w_kv, mem_qk_norm_q=mem_qk_norm_q,
        mem_qk_norm_k=mem_qk_norm_k, w_branch_gdn=w_branch_gdn, w_branch_mla=w_branch_mla,
        w_branch_mem=w_branch_mem, w_out=w_out, g_ffn=g_ffn, peer_w_q=peer_w_q,
        peer_keys_a=peer_keys_a, peer_keys_b=peer_keys_b)
    return (_layer(x_prompt, mem_prompt, wts), _layer(x_sample, mem_sample, wts))
```
